```python
import math
import jax, jax.numpy as jnp
from jax import lax
import numpy as np

D_MODEL = 4096
BATCH = 32
SEQ = 256
DEPTH = 2
DEC_BATCH = 2
DEC_SEQ = 1024
PAST_LEN = 256

GRID_W = 64
ATT_WIDTH = D_MODEL // 2
SSM_WIDTH = D_MODEL // 4
CONV_WIDTH = D_MODEL - ATT_WIDTH - SSM_WIDTH
ATT_HEAD_D = 64
N_ATT_HEADS = ATT_WIDTH // (2 * ATT_HEAD_D)
ROPE_AXIS_DIM = ATT_HEAD_D // 2
ROPE_BASE = 10000.0
Q_BLOCK = 128
SSM_GROUP = 16
N_SSM_GROUPS = SSM_WIDTH // SSM_GROUP
SSM_STATE = 64
CONV_K = 31
IN_WIDTH = 3 * ATT_WIDTH + SSM_WIDTH + 2 * CONV_WIDTH
N_EXPERTS = 128
TOP_K = 8
N_EXPERT_GROUPS = 8
TOPK_GROUPS = 4
EXPERT_FF = 512
SHARED_FF = 512
ROUTED_SCALE = 2.5
MOE_BLOCK = 128
EPS = 1e-6

kernel_name = "hybrid_s5_diffattn_conformer_moe_dit_step"

F32 = jnp.float32


def rms_norm(x, g):
    xf = x.astype(F32)
    y = xf * lax.rsqrt(jnp.mean(xf * xf, axis=-1, keepdims=True) + EPS)
    return (y * g.astype(F32)).astype(x.dtype)


def layer_norm(x, g, b):
    xf = x.astype(F32)
    mu = jnp.mean(xf, axis=-1, keepdims=True)
    xc = xf - mu
    var = jnp.mean(xc * xc, axis=-1, keepdims=True)
    return (xc * lax.rsqrt(var + EPS) * g.astype(F32) + b.astype(F32)).astype(x.dtype)


def adaln_params(cvec, w, b):
    m = jnp.dot(jax.nn.silu(cvec), w) + b
    return [t[..., None, :] for t in jnp.split(m, 6, axis=-1)]


def modulate(x, shift, scale):
    return x * (1.0 + scale) + shift


def axial_rope(n_tokens):
    rows = n_tokens // GRID_W
    row = jnp.repeat(jnp.arange(rows, dtype=F32), GRID_W)
    col = (jnp.arange(n_tokens) % GRID_W).astype(F32)
    n_freq = ROPE_AXIS_DIM // 2
    inv = ROPE_BASE ** (-jnp.arange(n_freq, dtype=F32) / n_freq)
    ang = jnp.stack([row[:, None] * inv, col[:, None] * inv], axis=1)
    return jnp.cos(ang), jnp.sin(ang)


def apply_rope(x, cos, sin):
    shp = x.shape
    xr = x.astype(F32).reshape(shp[:-1] + (2, 2, ROPE_AXIS_DIM // 2))
    x1, x2 = xr[..., 0, :], xr[..., 1, :]
    c = cos[:, None, None]
    s = sin[:, None, None]
    out = jnp.stack([x1 * c - x2 * s, x2 * c + x1 * s], axis=-2)
    return out.reshape(shp).astype(x.dtype)


def diff_attention(q, k, v, lam):
    bsz, lq = q.shape[:2]
    nblk = lq // Q_BLOCK
    qb = jnp.moveaxis(q.reshape((bsz, nblk, Q_BLOCK) + q.shape[2:]), 1, 0)
    scale = ATT_HEAD_D ** -0.5

    def block(qblk):
        s = jnp.einsum('bqhmd,bkhmd->bmhqk', qblk, k, preferred_element_type=F32) * scale
        p = jax.nn.softmax(s, axis=-1)
        pd = (p[:, 0] - lam * p[:, 1]).astype(v.dtype)
        return jnp.einsum('bhqk,bkhe->bqhe', pd, v)

    o = lax.map(block, qb)
    return jnp.moveaxis(o, 0, 1).reshape(bsz, lq, N_ATT_HEADS, 2 * ATT_HEAD_D)


def ssm_discretize(a_re, a_im, log_dt, b_re, b_im):
    dt = jnp.exp(log_dt.astype(F32))[:, None]
    lr = a_re.astype(F32) * dt
    li = a_im.astype(F32) * dt
    mag = jnp.exp(lr)
    ar, ai = mag * jnp.cos(li), mag * jnp.sin(li)
    zr, zi = ar - 1.0, ai
    den = lr * lr + li * li
    fr = (zr * lr + zi * li) / den * dt
    fi = (zi * lr - zr * li) / den * dt
    br, bi = b_re.astype(F32), b_im.astype(F32)
    bbr = fr[..., None] * br - fi[..., None] * bi
    bbi = fr[..., None] * bi + fi[..., None] * br
    return lr, li, ar, ai, bbr, bbi


def ssm_combine(lhs, rhs):
    a1r, a1i, b1r, b1i = lhs
    a2r, a2i, b2r, b2i = rhs
    return (a1r * a2r - a1i * a2i, a1r * a2i + a1i * a2r,
            a2r * b1r - a2i * b1i + b2r, a2r * b1i + a2i * b1r + b2i)


def ssm_direction(u, disc, c_re, c_im, h0, reverse):
    lr, li, ar, ai, bbr, bbi = disc
    if reverse:
        u = jnp.flip(u, axis=1)
    n_tok = u.shape[1]
    bur = jnp.einsum('blgh,gph->blgp', u, bbr)
    bui = jnp.einsum('blgh,gph->blgp', u, bbi)
    a_r = jnp.broadcast_to(ar, bur.shape)
    a_i = jnp.broadcast_to(ai, bur.shape)
    _, _, hr, hi = lax.associative_scan(ssm_combine, (a_r, a_i, bur, bui), axis=1)
    if h0 is not None:
        h0r, h0i = h0
        steps = jnp.arange(1, n_tok + 1, dtype=F32)[:, None, None]
        pmag = jnp.exp(lr * steps)
        ph = li * steps
        pr, pi = pmag * jnp.cos(ph), pmag * jnp.sin(ph)
        hr, hi = (hr + pr * h0r[:, None] - pi * h0i[:, None],
                  hi + pr * h0i[:, None] + pi * h0r[:, None])
    y = jnp.einsum('blgp,ghp->blgh', hr, c_re) - jnp.einsum('blgp,ghp->blgh', hi, c_im)
    final = (hr[:, -1], hi[:, -1])
    if reverse:
        y = jnp.flip(y, axis=1)
    return y, final


def ssm_mixer(u, p, h0):
    bsz, n_tok, _ = u.shape
    uf = u.astype(F32)
    ug = uf.reshape(bsz, n_tok, N_SSM_GROUPS, SSM_GROUP)
    y = p['ssm_d'].astype(F32) * uf
    fin_re, fin_im = [], []
    for d in range(2):
        disc = ssm_discretize(p['ssm_a_re'][d], p['ssm_a_im'][d], p['ssm_log_dt'][d],
                              p['ssm_b_re'][d], p['ssm_b_im'][d])
        h0_d = None if h0 is None else (h0[0][:, d].astype(F32), h0[1][:, d].astype(F32))
        y_d, (fr, fi) = ssm_direction(ug, disc, p['ssm_c_re'][d].astype(F32),
                                      p['ssm_c_im'][d].astype(F32), h0_d, reverse=(d == 1))
        y = y + y_d.reshape(bsz, n_tok, SSM_WIDTH)
        fin_re.append(fr)
        fin_im.append(fi)
    g = jax.nn.gelu(y).astype(u.dtype)
    out = g * jax.nn.sigmoid(jnp.dot(g, p['w_glu']) + p['b_glu'])
    return out, jnp.stack(fin_re, axis=1).astype(u.dtype), jnp.stack(fin_im, axis=1).astype(u.dtype)


def conv_mixer(z, p):
    a, g = jnp.split(z, 2, axis=-1)
    h = a * jax.nn.sigmoid(g)
    h = lax.conv_general_dilated(h, p['conv_dw'][:, None, :], window_strides=(1,),
                                 padding=[(CONV_K // 2, CONV_K // 2)],
                                 dimension_numbers=('NWC', 'WIO', 'NWC'),
                                 feature_group_count=CONV_WIDTH) + p['conv_dw_b']
    h = jax.nn.silu(layer_norm(h, p['conv_ln_g'], p['conv_ln_b']))
    return jnp.dot(h, p['w_pw']) + p['b_pw']


def mixer(h, p, layer_idx, ctx):
    bsz, n_tok, _ = h.shape
    proj = jnp.dot(h, p['w_in'])
    q, k, v, u, z = jnp.split(proj, [ATT_WIDTH, 2 * ATT_WIDTH, 3 * ATT_WIDTH,
                                     3 * ATT_WIDTH + SSM_WIDTH], axis=-1)
    q = rms_norm(q.reshape(bsz, n_tok, N_ATT_HEADS, 2, ATT_HEAD_D), p['q_norm'])
    k = rms_norm(k.reshape(bsz, n_tok, N_ATT_HEADS, 2, ATT_HEAD_D), p['k_norm'])
    v = v.reshape(bsz, n_tok, N_ATT_HEADS, 2 * ATT_HEAD_D)
    lam_init = 0.8 - 0.6 * math.exp(-0.3 * layer_idx)
    lam = (jnp.exp(jnp.sum(p['lam_q1'].astype(F32) * p['lam_k1'].astype(F32)))
           - jnp.exp(jnp.sum(p['lam_q2'].astype(F32) * p['lam_k2'].astype(F32))) + lam_init)
    if ctx is None:
        att = diff_attention(q, k, v, lam)
        s_out, s_re, s_im = ssm_mixer(u, p, None)
        new = (k, v, s_re, s_im)
    else:
        ck, cv, h0_re, h0_im = ctx
        cos, sin = axial_rope(n_tok)
        q = apply_rope(q, cos, sin)
        kr = apply_rope(k, cos, sin)
        att = diff_attention(q, jnp.concatenate([ck, kr], axis=1),
                             jnp.concatenate([cv, v], axis=1), lam)
        s_out, _, _ = ssm_mixer(u, p, (h0_re, h0_im))
        new = None
    att = (rms_norm(att, p['subln']) * (1.0 - lam_init)).reshape(bsz, n_tok, ATT_WIDTH)
    c_out = conv_mixer(z, p)
    out = jnp.dot(jnp.concatenate([att, s_out, c_out], axis=-1), p['w_out'])
    return out, new


def route(xf, w_r, bias):
    n_tok = xf.shape[0]
    per_group = N_EXPERTS // N_EXPERT_GROUPS
    scores = jax.nn.sigmoid(jnp.dot(xf, w_r, preferred_element_type=F32))
    sel = scores + bias.astype(F32)
    group_score = jnp.sum(lax.top_k(sel.reshape(n_tok, N_EXPERT_GROUPS, per_group), 2)[0], axis=-1)
    _, gidx = lax.top_k(group_score, TOPK_GROUPS)
    gmask = jnp.any(gidx[:, :, None] == jnp.arange(N_EXPERT_GROUPS)[None, None, :], axis=1)
    sel = jnp.where(jnp.repeat(gmask, per_group, axis=1), sel, -jnp.inf)
    _, eidx = lax.top_k(sel, TOP_K)
    w = jnp.take_along_axis(scores, eidx, axis=1)
    w = w / jnp.sum(w, axis=-1, keepdims=True) * ROUTED_SCALE
    return eidx, w


def moe_dispatch(xf, eidx, gw, w1, w3, w2, layer_idx):
    n_tok, dm = xf.shape
    n_assign = n_tok * TOP_K
    n_blocks = -(-n_assign // MOE_BLOCK) + N_EXPERTS
    flat_e = eidx.reshape(n_assign)
    flat_tok = jnp.repeat(jnp.arange(n_tok, dtype=jnp.int32), TOP_K)
    flat_w = gw.reshape(n_assign)
    order = jnp.argsort(flat_e)
    se = flat_e[order]
    counts = jnp.bincount(flat_e, length=N_EXPERTS)
    padded = (counts + MOE_BLOCK - 1) // MOE_BLOCK * MOE_BLOCK
    pad_end = jnp.cumsum(padded)
    starts = jnp.cumsum(counts) - counts
    dest = (pad_end - padded)[se] + jnp.arange(n_assign) - starts[se]
    rows_tok = jnp.zeros(n_blocks * MOE_BLOCK, jnp.int32).at[dest].set(flat_tok[order])
    rows_w = jnp.zeros(n_blocks * MOE_BLOCK, F32).at[dest].set(flat_w[order])
    block_e = jnp.minimum(jnp.searchsorted(pad_end, jnp.arange(n_blocks) * MOE_BLOCK, side='right'),
                          N_EXPERTS - 1)

    def expert_block(args):
        tok, e, wt = args
        xb = xf[tok]
        hb = jax.nn.silu(jnp.dot(xb, w1[layer_idx, e])) * jnp.dot(xb, w3[layer_idx, e])
        return jnp.dot(hb, w2[layer_idx, e]) * wt[:, None].astype(xf.dtype)

    yb = lax.map(expert_block, (rows_tok.reshape(n_blocks, MOE_BLOCK), block_e,
                                rows_w.reshape(n_blocks, MOE_BLOCK)))
    return jnp.zeros_like(xf).at[rows_tok].add(yb.reshape(-1, dm))


def moe(h, p, experts, layer_idx):
    bsz, n_tok, dm = h.shape
    xf = h.reshape(bsz * n_tok, dm)
    eidx, gw = route(xf, p['w_router'], p['router_bias'])
    w1, w3, w2 = experts
    routed = moe_dispatch(xf, eidx, gw, w1, w3, w2, layer_idx)
    shared = jnp.dot(jax.nn.silu(jnp.dot(xf, p['ws1'])) * jnp.dot(xf, p['ws3']), p['ws2'])
    return (routed + shared).reshape(bsz, n_tok, dm)


def layer(x, cvec, p, experts, layer_idx, ctx):
    s_m, sc_m, g_m, s_f, sc_f, g_f = adaln_params(cvec, p['w_ada'], p['b_ada'])
    h = modulate(rms_norm(x, p['norm_mix']), s_m, sc_m)
    mix, new = mixer(h, p, layer_idx, ctx)
    x = x + g_m * mix
    h = modulate(rms_norm(x, p['norm_ff']), s_f, sc_f)
    x = x + g_f * moe(h, p, experts, layer_idx)
    return x, new


def setup_inputs(seed: int = 0) -> dict:
    key = jax.random.key(seed)
    ks = iter(jax.random.split(key, 64))

    def nrm(shape, scale):
        return jax.random.normal(next(ks), shape, F32) * scale

    G, P, H = N_SSM_GROUPS, SSM_STATE, SSM_GROUP
    a_im = jnp.broadcast_to(jnp.pi * jnp.arange(P, dtype=F32), (DEPTH, 2, G, P))
    return {
        "x_prompt": nrm((BATCH, SEQ, D_MODEL), 1.0),
        "x_sample": nrm((DEC_BATCH, DEC_SEQ, D_MODEL), 1.0),
        "cache_k": nrm((DEC_BATCH, DEPTH, PAST_LEN, N_ATT_HEADS, 2, ATT_HEAD_D), 1.0),
        "cache_v": nrm((DEC_BATCH, DEPTH, PAST_LEN, N_ATT_HEADS, 2 * ATT_HEAD_D), 1.0),
        "state_ssm_re": nrm((DEC_BATCH, DEPTH, 2, G, P), 0.1),
        "state_ssm_im": nrm((DEC_BATCH, DEPTH, 2, G, P), 0.1),
        "c": nrm((DEC_BATCH, D_MODEL), 1.0),
        "c_ctx": nrm((D_MODEL,), 1.0),
        "w_ada": nrm((DEPTH, D_MODEL, 6 * D_MODEL), 0.5 * D_MODEL ** -0.5),
        "b_ada": nrm((DEPTH, 6 * D_MODEL), 0.01),
        "norm_mix": 1.0 + nrm((DEPTH, D_MODEL), 0.01),
        "norm_ff": 1.0 + nrm((DEPTH, D_MODEL), 0.01),
        "w_in": nrm((DEPTH, D_MODEL, IN_WIDTH), D_MODEL ** -0.5),
        "w_out": nrm((DEPTH, D_MODEL, D_MODEL), D_MODEL ** -0.5),
        "q_norm": 1.0 + nrm((DEPTH, ATT_HEAD_D), 0.01),
        "k_norm": 1.0 + nrm((DEPTH, ATT_HEAD_D), 0.01),
        "lam_q1": nrm((DEPTH, ATT_HEAD_D), 0.1),
        "lam_k1": nrm((DEPTH, ATT_HEAD_D), 0.1),
        "lam_q2": nrm((DEPTH, ATT_HEAD_D), 0.1),
        "lam_k2": nrm((DEPTH, ATT_HEAD_D), 0.1),
        "subln": 1.0 + nrm((DEPTH, 2 * ATT_HEAD_D), 0.01),
        "ssm_a_re": -0.5 + nrm((DEPTH, 2, G, P), 0.01),
        "ssm_a_im": a_im + nrm((DEPTH, 2, G, P), 0.01),
        "ssm_log_dt": jax.random.uniform(next(ks), (DEPTH, 2, G), F32,
                                         minval=math.log(1e-3), maxval=math.log(1e-1)),
        "ssm_b_re": nrm((DEPTH, 2, G, P, H), (2 * H) ** -0.5),
        "ssm_b_im": nrm((DEPTH, 2, G, P, H), (2 * H) ** -0.5),
        "ssm_c_re": nrm((DEPTH, 2, G, H, P), (2 * P) ** -0.5),
        "ssm_c_im": nrm((DEPTH, 2, G, H, P), (2 * P) ** -0.5),
        "ssm_d": nrm((DEPTH, SSM_WIDTH), 1.0),
        "w_glu": nrm((DEPTH, SSM_WIDTH, SSM_WIDTH), SSM_WIDTH ** -0.5),
        "b_glu": nrm((DEPTH, SSM_WIDTH), 0.01),
        "conv_dw": nrm((DEPTH, CONV_K, CONV_WIDTH), CONV_K ** -0.5),
        "conv_dw_b": nrm((DEPTH, CONV_WIDTH), 0.01),
        "conv_ln_g": 1.0 + nrm((DEPTH, CONV_WIDTH), 0.01),
        "conv_ln_b": nrm((DEPTH, CONV_WIDTH), 0.01),
        "w_pw": nrm((DEPTH, CONV_WIDTH, CONV_WIDTH), CONV_WIDTH ** -0.5),
        "b_pw": nrm((DEPTH, CONV_WIDTH), 0.01),
        "w_router": nrm((DEPTH, D_MODEL, N_EXPERTS), D_MODEL ** -0.5),
        "router_bias": nrm((DEPTH, N_EXPERTS), 0.01),
        "w1": nrm((DEPTH, N_EXPERTS, D_MODEL, EXPERT_FF), D_MODEL ** -0.5),
        "w3": nrm((DEPTH, N_EXPERTS, D_MODEL, EXPERT_FF), D_MODEL ** -0.5),
        "w2": nrm((DEPTH, N_EXPERTS, EXPERT_FF, D_MODEL), EXPERT_FF ** -0.5),
        "ws1": nrm((DEPTH, D_MODEL, SHARED_FF), D_MODEL ** -0.5),
        "ws3": nrm((DEPTH, D_MODEL, SHARED_FF), D_MODEL ** -0.5),
        "ws2": nrm((DEPTH, SHARED_FF, D_MODEL), SHARED_FF ** -0.5),
    }


def reference(x_prompt, x_sample, cache_k, cache_v, state_ssm_re, state_ssm_im, c, c_ctx,
              w_ada, b_ada, norm_mix, norm_ff, w_in, w_out, q_norm, k_norm,
              lam_q1, lam_k1, lam_q2, lam_k2, subln,
              ssm_a_re, ssm_a_im, ssm_log_dt, ssm_b_re, ssm_b_im, ssm_c_re, ssm_c_im, ssm_d,
              w_glu, b_glu, conv_dw, conv_dw_b, conv_ln_g, conv_ln_b, w_pw, b_pw,
              w_router, router_bias, w1, w3, w2, ws1, ws3, ws2):
    experts = (w1, w3, w2)
    xp, xs = x_prompt, x_sample
    ks_, vs_, srs, sis = [], [], [], []
    for l in range(DEPTH):
        p = {
            'w_ada': w_ada[l], 'b_ada': b_ada[l], 'norm_mix': norm_mix[l], 'norm_ff': norm_ff[l],
            'w_in': w_in[l], 'w_out': w_out[l], 'q_norm': q_norm[l], 'k_norm': k_norm[l],
            'lam_q1': lam_q1[l], 'lam_k1': lam_k1[l], 'lam_q2': lam_q2[l], 'lam_k2': lam_k2[l],
            'subln': subln[l], 'ssm_a_re': ssm_a_re[l], 'ssm_a_im': ssm_a_im[l],
            'ssm_log_dt': ssm_log_dt[l], 'ssm_b_re': ssm_b_re[l], 'ssm_b_im': ssm_b_im[l],
            'ssm_c_re': ssm_c_re[l], 'ssm_c_im': ssm_c_im[l], 'ssm_d': ssm_d[l],
            'w_glu': w_glu[l], 'b_glu': b_glu[l], 'conv_dw': conv_dw[l], 'conv_dw_b': conv_dw_b[l],
            'conv_ln_g': conv_ln_g[l], 'conv_ln_b': conv_ln_b[l], 'w_pw': w_pw[l], 'b_pw': b_pw[l],
            'w_router': w_router[l], 'router_bias': router_bias[l],
            'ws1': ws1[l], 'ws3': ws3[l], 'ws2': ws2[l],
        }
        xp, (k_l, v_l, sr_l, si_l) = layer(xp, c_ctx, p, experts, l, None)
        ks_.append(k_l)
        vs_.append(v_l)
        srs.append(sr_l)
        sis.append(si_l)
        xs, _ = layer(xs, c, p, experts, l,
                      (cache_k[:, l], cache_v[:, l], state_ssm_re[:, l], state_ssm_im[:, l]))
    new_cache_k = jnp.stack(ks_, axis=1)
    new_cache_v = jnp.stack(vs_, axis=1)
    new_state_ssm_re = jnp.stack(srs, axis=1)
    new_state_ssm_im = jnp.stack(sis, axis=1)
    return (xp, xs, new_cache_k, new_cache_v, new_state_ssm_re, new_state_ssm_im)
```

```python
import functools
import math

import jax
import jax.numpy as jnp
from jax import lax
from jax.experimental import pallas as pl
from jax.experimental.pallas import tpu as pltpu

F32 = jnp.float32
BF16 = jnp.bfloat16
U32 = jnp.uint32
I32 = jnp.int32

EPS = 1e-6
GRID_W = 64
ROPE_BASE = 10000.0
TOP_K = 8
N_EXPERT_GROUPS = 8
TOPK_GROUPS = 4
ROUTED_SCALE = 2.5

LANES = 128
SUBLANES = 8
VMEM_LIMIT = 56 * 1024 * 1024
SSM_SLICE_GROUPS = 8
MOE_ROWS = 256
DMA_RING = 256


def _cparams(sem, vmem=VMEM_LIMIT):
    return pltpu.CompilerParams(dimension_semantics=sem, vmem_limit_bytes=vmem)


def _sigmoid(x):
    return 1.0 / (1.0 + jnp.exp(-x))


def _silu(x):
    return x * _sigmoid(x)


def _adaln_body(c_ref, w_ref, b_ref, o_ref):
    c = c_ref[...]
    s = _silu(c).astype(BF16)
    o_ref[0] = jnp.dot(s, w_ref[0].astype(BF16), preferred_element_type=F32) + b_ref[0]


def _adaln(cmat, w_ada, b_ada):
    depth, d, n = w_ada.shape
    tn = 512
    return pl.pallas_call(
        _adaln_body,
        grid=(depth, n // tn),
        in_specs=[
            pl.BlockSpec((SUBLANES, d), lambda l, j: (0, 0)),
            pl.BlockSpec((1, d, tn), lambda l, j: (l, 0, j)),
            pl.BlockSpec((1, 1, tn), lambda l, j: (l, 0, j)),
        ],
        out_specs=pl.BlockSpec((1, SUBLANES, tn), lambda l, j: (l, 0, j)),
        out_shape=jax.ShapeDtypeStruct((depth, SUBLANES, n), F32),
        compiler_params=_cparams(("parallel", "parallel")),
        name="adaln",
    )(cmat, w_ada, b_ada.reshape(depth, 1, n))


def _norm_mod(x, nw, shift, scale):
    ms = jnp.mean(x * x, axis=-1, keepdims=True)
    y = x * lax.rsqrt(ms + EPS) * nw
    return y * (1.0 + scale) + shift


def _pack_bf16_pair(a, b):
    ab = pltpu.bitcast(a.astype(BF16).astype(F32), U32)
    bb = pltpu.bitcast(b.astype(BF16).astype(F32), U32)
    return (ab & jnp.uint32(0xFFFF0000)) | (bb >> 16)


def _unpack_bf16_pair(p):
    a = pltpu.bitcast(p & jnp.uint32(0xFFFF0000), F32).astype(BF16)
    b = pltpu.bitcast(p << 16, F32).astype(BF16)
    return a, b


def _mod_row_fn(n_prompt_tiles, tiles_per_dec_batch):
    def row(i):
        return jnp.where(i < n_prompt_tiles, 0, 1 + (i - n_prompt_tiles) // tiles_per_dec_batch)
    return row


def _in_proj_body(x_ref, nw_ref, mod_ref, w_ref, o_ref, h_scr):
    @pl.when(pl.program_id(1) == 0)
    def _():
        h = _norm_mod(x_ref[...], nw_ref[...], mod_ref[0:1, :], mod_ref[1:2, :])
        h_scr[...] = h.astype(BF16)

    o_ref[...] = jnp.dot(h_scr[...], w_ref[...], preferred_element_type=F32)


def _in_proj(x, nw, mod_l, w_bf16, row_fn, tm):
    t, d = x.shape
    n = w_bf16.shape[1]
    tn = 512 if n % 512 == 0 else LANES
    return pl.pallas_call(
        _in_proj_body,
        grid=(t // tm, n // tn),
        in_specs=[
            pl.BlockSpec((tm, d), lambda i, j: (i, 0)),
            pl.BlockSpec((1, d), lambda i, j: (0, 0)),
            pl.BlockSpec((None, 6, d), lambda i, j: (row_fn(i), 0, 0)),
            pl.BlockSpec((d, tn), lambda i, j: (0, j)),
        ],
        out_specs=pl.BlockSpec((tm, tn), lambda i, j: (i, j)),
        out_shape=jax.ShapeDtypeStruct((t, n), F32),
        scratch_shapes=[pltpu.VMEM((tm, d), BF16)],
        compiler_params=_cparams(("parallel", "arbitrary")),
        name="in_proj",
    )(x, nw.reshape(1, d), mod_l, w_bf16)


def _half_norm(x, w):
    lane = lax.broadcasted_iota(I32, (1, LANES), 1)
    lo = lane < (LANES // 2)
    x2 = x * x
    s_lo = jnp.sum(jnp.where(lo, x2, 0.0), axis=-1, keepdims=True)
    s_hi = jnp.sum(jnp.where(lo, 0.0, x2), axis=-1, keepdims=True)
    ss = jnp.where(lo, s_lo, s_hi)
    return x * lax.rsqrt(ss * (2.0 / LANES) + EPS) * w


def _rope(x, cos, sa, sb):
    return x * cos + pltpu.roll(x, LANES - 16, 1) * sa + pltpu.roll(x, 16, 1) * sb


def _softmax(s):
    m = jnp.max(s, axis=-1, keepdims=True)
    e = jnp.exp(s - m)
    return e / jnp.sum(e, axis=-1, keepdims=True)


def _attn_body(*refs, seq, ctx_len, heads, q_chunk, out_scale):
    has_ctx = ctx_len > 0
    if has_ctx:
        (lam_ref, q_ref, k_ref, v_ref, qn_ref, kn_ref, sub_ref, ck_ref, cv_ref, cos_ref, sa_ref, sb_ref,
         att_ref, kcat, vcat) = refs
    else:
        lam_ref, q_ref, k_ref, v_ref, qn_ref, kn_ref, sub_ref, att_ref, ko_ref, vo_ref = refs
    lam = lam_ref[0]
    lane = lax.broadcasted_iota(I32, (1, LANES), 1)
    lo = lane < (LANES // 2)
    scale = (LANES // 2) ** -0.5
    for j in range(heads):
        sl = slice(j * LANES, (j + 1) * LANES)
        kn = _half_norm(k_ref[:, sl], kn_ref[...])
        if has_ctx:
            kn = _rope(kn, cos_ref[...], sa_ref[...], sb_ref[...])
            kcat[0:ctx_len, :] = ck_ref[:, sl].astype(BF16)
            kcat[ctx_len:, :] = kn.astype(BF16)
            vcat[0:ctx_len, :] = cv_ref[:, sl].astype(BF16)
            vcat[ctx_len:, :] = v_ref[:, sl].astype(BF16)
            kfull = kcat[...]
            vfull = vcat[...]
        else:
            ko_ref[:, sl] = kn
            vo_ref[:, sl] = v_ref[:, sl]
            kfull = kn.astype(BF16)
            vfull = v_ref[:, sl].astype(BF16)
        for c in range(seq // q_chunk):
            rows = slice(c * q_chunk, (c + 1) * q_chunk)
            qn = _half_norm(q_ref[rows, sl], qn_ref[...])
            if has_ctx:
                qn = _rope(qn, cos_ref[rows, :], sa_ref[rows, :], sb_ref[rows, :])
            qn = qn * scale
            q1 = jnp.where(lo, qn, 0.0).astype(BF16)
            q2 = jnp.where(lo, 0.0, qn).astype(BF16)
            dn = (((1,), (1,)), ((), ()))
            p1 = _softmax(lax.dot_general(q1, kfull, dn, preferred_element_type=F32))
            p2 = _softmax(lax.dot_general(q2, kfull, dn, preferred_element_type=F32))
            pd = (p1 - lam * p2).astype(BF16)
            o = jnp.dot(pd, vfull, preferred_element_type=F32)
            o = o * lax.rsqrt(jnp.mean(o * o, axis=-1, keepdims=True) + EPS) * sub_ref[...] * out_scale
            att_ref[rows, sl] = o.astype(BF16)


def _attention(proj, lam, qn_w, kn_w, sub_w, *, n_batch, seq, row_block0, att_width, out_scale,
               ctx=None, rope=None):
    n_heads = att_width // LANES
    heads = min(n_heads, 4) if ctx is not None else n_heads
    hw = heads * LANES
    kcol = att_width // hw
    q_chunk = min(seq, 256)
    rep = lambda w: jnp.tile(w.reshape(1, -1), (1, LANES // w.shape[-1])).astype(F32)
    smem = pl.BlockSpec(memory_space=pltpu.SMEM)
    vec = pl.BlockSpec((1, LANES), lambda b, h: (0, 0))
    in_specs = [
        smem,
        pl.BlockSpec((seq, hw), lambda b, h: (row_block0 + b, h)),
        pl.BlockSpec((seq, hw), lambda b, h: (row_block0 + b, kcol + h)),
        pl.BlockSpec((seq, hw), lambda b, h: (row_block0 + b, 2 * kcol + h)),
        vec, vec, vec,
    ]
    args = [lam.reshape(1), proj, proj, proj, rep(qn_w), rep(kn_w), rep(sub_w)]
    att_spec = pl.BlockSpec((seq, hw), lambda b, h: (b, h))
    att_shape = jax.ShapeDtypeStruct((n_batch * seq, att_width), BF16)
    if ctx is None:
        ctx_len = 0
        out_specs = [att_spec, att_spec, att_spec]
        out_shape = [att_shape, jax.ShapeDtypeStruct((n_batch * seq, att_width), F32),
                     jax.ShapeDtypeStruct((n_batch * seq, att_width), F32)]
        scratch = []
    else:
        ck, cv = ctx
        ctx_len = ck.shape[1]
        cspec = pl.BlockSpec((None, ctx_len, hw), lambda b, h: (b, 0, h))
        tab = pl.BlockSpec((seq, LANES), lambda b, h: (0, 0))
        in_specs += [cspec, cspec, tab, tab, tab]
        args += [ck, cv, *rope]
        out_specs = att_spec
        out_shape = att_shape
        scratch = [pltpu.VMEM((ctx_len + seq, LANES), BF16), pltpu.VMEM((ctx_len + seq, LANES), BF16)]
    body = functools.partial(_attn_body, seq=seq, ctx_len=ctx_len, heads=heads, q_chunk=q_chunk,
                             out_scale=out_scale)
    return pl.pallas_call(
        body,
        grid=(n_batch, n_heads // heads),
        in_specs=in_specs,
        out_specs=out_specs,
        out_shape=out_shape,
        scratch_shapes=scratch,
        compiler_params=_cparams(("parallel", "parallel")),
        name="attn_ctx" if ctx is None else "attn_dec",
    )(*args)


def _rope_tables(n_tok, head_d):
    axis_dim = head_d // 2
    n_freq = axis_dim // 2
    pos = jnp.arange(n_tok)
    row = (pos // GRID_W).astype(F32)
    col = (pos % GRID_W).astype(F32)
    inv = ROPE_BASE ** (-jnp.arange(n_freq, dtype=F32) / n_freq)
    ang = jnp.stack([row[:, None] * inv, col[:, None] * inv], axis=1)
    cos = jnp.broadcast_to(jnp.cos(ang)[:, :, None, :], (n_tok, 2, 2, n_freq))
    sin = jnp.broadcast_to(jnp.sin(ang)[:, :, None, :], (n_tok, 2, 2, n_freq))
    first = (jnp.arange(2) == 0)[None, None, :, None]
    sa = jnp.where(first, -sin, 0.0)
    sb = jnp.where(first, 0.0, sin)
    flat = lambda t: jnp.tile(t.reshape(n_tok, head_d), (1, LANES // head_d))
    return flat(cos), flat(sa), flat(sb)


def _ssm_tables(a_re, a_im, log_dt, b_re, b_im, c_re, c_im):
    _, g, p = a_re.shape
    hch = b_re.shape[-1]
    gs = SSM_SLICE_GROUPS
    s = g // gs
    dt = jnp.exp(log_dt.astype(F32))[..., None]
    lr = a_re.astype(F32) * dt
    li = a_im.astype(F32) * dt
    mag = jnp.exp(lr)
    ar, ai = mag * jnp.cos(li), mag * jnp.sin(li)
    zr, zi = ar - 1.0, ai
    den = lr * lr + li * li
    fr = (zr * lr + zi * li) / den * dt
    fi = (zi * lr - zr * li) / den * dt
    bbr = fr[..., None] * b_re - fi[..., None] * b_im
    bbi = fr[..., None] * b_im + fi[..., None] * b_re
    eye = jnp.eye(gs, dtype=F32)

    def bdiag_b(m):
        m = m.reshape(2, s, gs, p, hch)
        return jnp.einsum('dsgph,gk->dsghkp', m, eye).reshape(2, s, gs * hch, gs * p)

    def bdiag_c(m):
        m = m.reshape(2, s, gs, hch, p)
        return jnp.einsum('dsghp,gk->dsgpkh', m, eye).reshape(2, s, gs * p, gs * hch)

    bblk = jnp.concatenate([bdiag_b(bbr), bdiag_b(bbi)], axis=-1).astype(BF16)
    cblk = jnp.concatenate([bdiag_c(c_re.astype(F32)), bdiag_c(-c_im.astype(F32))], axis=-2).astype(BF16)
    a = jnp.stack([ar.reshape(2, s, gs * p), ai.reshape(2, s, gs * p)], axis=2)
    return a, bblk, cblk


def _ssm_body(*refs, nb, tc, gw, has_h0):
    if has_h0:
        u_ref, b_ref, c_ref, a_ref, h0r_ref, h0i_ref, y_ref, s_scr, h_scr = refs
    else:
        u_ref, b_ref, c_ref, a_ref, y_ref, fr_ref, fi_ref, s_scr, h_scr = refs
    d = pl.program_id(0)
    c = pl.program_id(3)

    nt = gw // LANES
    tile = lambda j: slice(j * LANES, (j + 1) * LANES)

    @pl.when(c == 0)
    def _():
        for j in range(nt):
            if has_h0:
                h_scr[j] = h0r_ref[:, tile(j)]
                h_scr[nt + j] = h0i_ref[:, tile(j)]
            else:
                h_scr[j] = jnp.zeros((nb, LANES), F32)
                h_scr[nt + j] = jnp.zeros((nb, LANES), F32)

    u = u_ref[...].reshape(nb * tc, LANES).astype(BF16)
    bu = jnp.dot(u, b_ref[...], preferred_element_type=F32)
    for j in range(2 * nt):
        s_scr[j] = bu[:, tile(j)]
    ar = [jnp.broadcast_to(a_ref[0:1, tile(j)], (nb, LANES)) for j in range(nt)]
    ai = [jnp.broadcast_to(a_ref[1:2, tile(j)], (nb, LANES)) for j in range(nt)]

    def step(t, carry):
        te = jnp.where(d == 0, t, tc - 1 - t)
        rows = pl.ds(te, nb, stride=tc)
        out = []
        for j in range(nt):
            hr, hi = carry[j], carry[nt + j]
            nhr = ar[j] * hr - ai[j] * hi + s_scr[j, rows, :]
            nhi = ar[j] * hi + ai[j] * hr + s_scr[nt + j, rows, :]
            s_scr[j, rows, :] = nhr
            s_scr[nt + j, rows, :] = nhi
            out.append((nhr, nhi))
        return tuple(o[0] for o in out) + tuple(o[1] for o in out)

    h = lax.fori_loop(0, tc, step, tuple(h_scr[j] for j in range(2 * nt)))
    for j in range(2 * nt):
        h_scr[j] = h[j]
    hs = jnp.concatenate([s_scr[j] for j in range(2 * nt)], axis=-1).astype(BF16)
    y = jnp.dot(hs, c_ref[...], preferred_element_type=F32)
    y_ref[...] = y.reshape(nb, tc, LANES)
    if not has_h0:
        for j in range(nt):
            fr_ref[:, tile(j)] = h[j]
            fi_ref[:, tile(j)] = h[nt + j]


def _ssm_scan(proj3, tables, *, n_batch, seq, batch_block0, ucol0, nb, h0=None):
    a, bblk, cblk = tables
    _, s, _, gw = a.shape
    tc = min(seq, 256)
    n_chunks = seq // tc
    ucol = ucol0 // LANES
    ssm_w = s * LANES
    cidx = lambda d, c: jnp.where(d == 0, c, n_chunks - 1 - c)
    in_specs = [
        pl.BlockSpec((nb, tc, LANES), lambda d, b, g, c: (batch_block0 + b, cidx(d, c), ucol + g)),
        pl.BlockSpec((None, None, LANES, 2 * gw), lambda d, b, g, c: (d, g, 0, 0)),
        pl.BlockSpec((None, None, 2 * gw, LANES), lambda d, b, g, c: (d, g, 0, 0)),
        pl.BlockSpec((None, None, 2, gw), lambda d, b, g, c: (d, g, 0, 0)),
    ]
    args = [proj3, bblk, cblk, a]
    y_spec = pl.BlockSpec((None, nb, tc, LANES), lambda d, b, g, c: (d, b, cidx(d, c), g))
    y_shape = jax.ShapeDtypeStruct((2, n_batch, seq, ssm_w), F32)
    st_spec = pl.BlockSpec((None, None, nb, gw), lambda d, b, g, c: (d, g, b, 0))
    if h0 is None:
        out_specs = [y_spec, st_spec, st_spec]
        st_shape = jax.ShapeDtypeStruct((2, s, n_batch, gw), F32)
        out_shape = [y_shape, st_shape, st_shape]
    else:
        in_specs += [st_spec, st_spec]
        args += list(h0)
        out_specs = y_spec
        out_shape = y_shape
    body = functools.partial(_ssm_body, nb=nb, tc=tc, gw=gw, has_h0=h0 is not None)
    return pl.pallas_call(
        body,
        grid=(2, n_batch // nb, s, n_chunks),
        in_specs=in_specs,
        out_specs=out_specs,
        out_shape=out_shape,
        scratch_shapes=[pltpu.VMEM((2 * gw // LANES, nb * tc, LANES), F32),
                        pltpu.VMEM((2 * gw // LANES, nb, LANES), F32)],
        compiler_params=_cparams(("parallel", "parallel", "parallel", "arbitrary")),
        name="ssm_scan",
    )(*args)


def _gelu_tanh(x):
    return 0.5 * x * (1.0 + jnp.tanh(math.sqrt(2.0 / math.pi) * (x + 0.044715 * x * x * x)))


def _ssm_out_body(u_ref, y_ref, d_ref, w_ref, b_ref, o_ref):
    y = d_ref[...] * u_ref[...] + y_ref[0] + y_ref[1]
    g = _gelu_tanh(y).astype(BF16)
    z = jnp.dot(g, w_ref[...], preferred_element_type=F32) + b_ref[...]
    o_ref[...] = (g.astype(F32) * _sigmoid(z)).astype(BF16)


def _ssm_out(proj, y2, d_vec, w_glu_bf16, b_glu, *, row_block0, ucol0, tm):
    _, rows, w = y2.shape
    ucol = ucol0 // w
    return pl.pallas_call(
        _ssm_out_body,
        grid=(rows // tm,),
        in_specs=[
            pl.BlockSpec((tm, w), lambda i: (row_block0 + i, ucol)),
            pl.BlockSpec((2, tm, w), lambda i: (0, i, 0)),
            pl.BlockSpec((1, w), lambda i: (0, 0)),
            pl.BlockSpec((w, w), lambda i: (0, 0)),
            pl.BlockSpec((1, w), lambda i: (0, 0)),
        ],
        out_specs=pl.BlockSpec((tm, w), lambda i: (i, 0)),
        out_shape=jax.ShapeDtypeStruct((rows, w), BF16),
        compiler_params=_cparams(("parallel",)),
        name="ssm_out",
    )(proj, y2, d_vec.reshape(1, w), w_glu_bf16, b_glu.reshape(1, w))


def _conv_body(a_ref, g_ref, dw_ref, dwb_ref, lng_ref, lnb_ref, w_ref, b_ref, o_ref, pad_scr, hs_scr, *, seq, taps,
               halo, row_chunk):
    w = a_ref.shape[-1]
    pad_scr[0:halo, :] = jnp.zeros((halo, w), F32)
    pad_scr[halo + seq:, :] = jnp.zeros((halo, w), F32)
    pad_scr[halo:halo + seq, :] = a_ref[...] * _sigmoid(g_ref[...])
    first = halo - taps // 2

    def chunk(r, _):
        r0 = pl.multiple_of(r * row_chunk, row_chunk)
        accs = []
        for lt in range(w // LANES):
            cols = slice(lt * LANES, (lt + 1) * LANES)
            win = pad_scr[pl.ds(r0, row_chunk + 2 * halo), cols]
            acc = jnp.zeros((row_chunk, LANES), F32) + dwb_ref[:, cols]
            for k in range(taps):
                acc = acc + win[first + k:first + k + row_chunk, :] * dw_ref[k:k + 1, cols]
            accs.append(acc)
        acc = jnp.concatenate(accs, axis=-1)
        mu = jnp.mean(acc, axis=-1, keepdims=True)
        xc = acc - mu
        var = jnp.mean(xc * xc, axis=-1, keepdims=True)
        hn = xc * lax.rsqrt(var + EPS) * lng_ref[...] + lnb_ref[...]
        hs_scr[pl.ds(r0, row_chunk), :] = _silu(hn).astype(BF16)
        return 0

    lax.fori_loop(0, seq // row_chunk, chunk, 0)
    out = jnp.dot(hs_scr[...], w_ref[...], preferred_element_type=F32) + b_ref[...]
    o_ref[...] = out.astype(BF16)


def _conv_mixer(proj, dw, dw_b, ln_g, ln_b, w_pw_bf16, b_pw, *, n_batch, seq, row_block0, zcol0):
    taps, w = dw.shape
    halo = 16
    acol = zcol0 // w
    row = lambda v: v.reshape(1, w)
    vec = pl.BlockSpec((1, w), lambda b: (0, 0))
    body = functools.partial(_conv_body, seq=seq, taps=taps, halo=halo, row_chunk=32)
    return pl.pallas_call(
        body,
        grid=(n_batch,),
        in_specs=[
            pl.BlockSpec((seq, w), lambda b: (row_block0 + b, acol)),
            pl.BlockSpec((seq, w), lambda b: (row_block0 + b, acol + 1)),
            pl.BlockSpec((taps, w), lambda b: (0, 0)),
            vec, vec, vec,
            pl.BlockSpec((w, w), lambda b: (0, 0)),
            vec,
        ],
        out_specs=pl.BlockSpec((seq, w), lambda b: (b, 0)),
        out_shape=jax.ShapeDtypeStruct((n_batch * seq, w), BF16),
        scratch_shapes=[pltpu.VMEM((seq + 2 * halo, w), F32), pltpu.VMEM((seq, w), BF16)],
        compiler_params=_cparams(("parallel",)),
        name="conv_mixer",
    )(proj, proj, dw, row(dw_b), row(ln_g), row(ln_b), w_pw_bf16, row(b_pw))


def _out_proj_body(att_ref, s_ref, c_ref, w_ref, x_ref, mod_ref, o_ref):
    wa = att_ref.shape[-1]
    ws = s_ref.shape[-1]
    mix = jnp.dot(att_ref[...], w_ref[0:wa, :], preferred_element_type=F32)
    mix = mix + jnp.dot(s_ref[...], w_ref[wa:wa + ws, :], preferred_element_type=F32)
    mix = mix + jnp.dot(c_ref[...], w_ref[wa + ws:, :], preferred_element_type=F32)
    o_ref[...] = x_ref[...] + mod_ref[2:3, :] * mix


def _out_proj(att, s_out, c_out, w_bf16, x, mod_l, row_fn, tm):
    t, d = x.shape
    tn = 512
    return pl.pallas_call(
        _out_proj_body,
        grid=(t // tm, d // tn),
        in_specs=[
            pl.BlockSpec((tm, att.shape[1]), lambda i, j: (i, 0)),
            pl.BlockSpec((tm, s_out.shape[1]), lambda i, j: (i, 0)),
            pl.BlockSpec((tm, c_out.shape[1]), lambda i, j: (i, 0)),
            pl.BlockSpec((d, tn), lambda i, j: (0, j)),
            pl.BlockSpec((tm, tn), lambda i, j: (i, j)),
            pl.BlockSpec((None, 6, tn), lambda i, j: (row_fn(i), 0, j)),
        ],
        out_specs=pl.BlockSpec((tm, tn), lambda i, j: (i, j)),
        out_shape=jax.ShapeDtypeStruct((t, d), F32),
        compiler_params=_cparams(("parallel", "parallel")),
        name="out_proj",
    )(att, s_out, c_out, w_bf16, x, mod_l)


def _route_body(x_ref, nw_ref, mod_ref, wrt_ref, bias_ref, tri_ref, hp_ref, eidx_ref, rank_ref, wgt_ref, cnt_ref,
                cnt_scr):
    i = pl.program_id(0)

    @pl.when(i == 0)
    def _():
        cnt_scr[...] = jnp.zeros_like(cnt_scr)

    h = _norm_mod(x_ref[...], nw_ref[...], mod_ref[3:4, :], mod_ref[4:5, :])
    half = h.shape[1] // 2
    hp_ref[...] = _pack_bf16_pair(h[:, :half], h[:, half:])

    n_exp = wrt_ref.shape[0]
    tm = h.shape[0]
    per_group = n_exp // N_EXPERT_GROUPS
    logits = lax.dot_general(wrt_ref[...], h, (((1,), (1,)), ((), ())), precision=lax.Precision.HIGHEST,
                             preferred_element_type=F32)
    scores = _sigmoid(logits)
    sel = scores + bias_ref[...]
    neg = -jnp.inf

    sel3 = sel.reshape(N_EXPERT_GROUPS, per_group, tm)
    pos3 = lax.broadcasted_iota(I32, sel3.shape, 1)
    m1 = jnp.max(sel3, axis=1, keepdims=True)
    first1 = jnp.min(jnp.where(sel3 == m1, pos3, per_group), axis=1, keepdims=True)
    m2 = jnp.max(jnp.where(pos3 == first1, neg, sel3), axis=1, keepdims=True)
    gscore = m1 + m2

    gpos = lax.broadcasted_iota(I32, gscore.shape, 0)
    gkeep = jnp.zeros(gscore.shape, F32)
    for _ in range(TOPK_GROUPS):
        gm = jnp.max(gscore, axis=0, keepdims=True)
        gfirst = jnp.min(jnp.where(gscore == gm, gpos, N_EXPERT_GROUPS), axis=0, keepdims=True)
        hit = gpos == gfirst
        gkeep = jnp.where(hit, 1.0, gkeep)
        gscore = jnp.where(hit, neg, gscore)
    keep3 = jnp.broadcast_to(gkeep, sel3.shape) > 0.5
    cand = jnp.where(keep3, sel3, neg).reshape(n_exp, tm)

    epos = lax.broadcasted_iota(I32, cand.shape, 0)
    picks = []
    chosen = jnp.zeros(cand.shape, F32)
    for _ in range(TOP_K):
        em = jnp.max(cand, axis=0, keepdims=True)
        efirst = jnp.min(jnp.where(cand == em, epos, n_exp), axis=0, keepdims=True)
        hit = epos == efirst
        chosen = jnp.where(hit, 1.0, chosen)
        cand = jnp.where(hit, neg, cand)
        picks.append(efirst)

    gate = chosen * scores
    gate = gate / jnp.sum(gate, axis=0, keepdims=True) * ROUTED_SCALE

    rank = jnp.dot(chosen.astype(BF16), tri_ref[...], preferred_element_type=F32) + cnt_scr[...]
    cnt_scr[...] = cnt_scr[...] + jnp.sum(chosen, axis=1, keepdims=True)
    cnt_ref[...] = cnt_scr[...].astype(I32)

    for k in range(TOP_K):
        hit = epos == picks[k]
        eidx_ref[k:k + 1, :] = picks[k]
        rank_ref[k:k + 1, :] = jnp.sum(jnp.where(hit, rank, 0.0), axis=0, keepdims=True).astype(I32)
        wgt_ref[k:k + 1, :] = jnp.sum(jnp.where(hit, gate, 0.0), axis=0, keepdims=True)


def _route(x, nw, mod_l, w_router, bias, row_fn, tm):
    t, d = x.shape
    n_exp = w_router.shape[1]
    tri = (jnp.arange(tm)[:, None] < jnp.arange(tm)[None, :]).astype(BF16)
    return pl.pallas_call(
        _route_body,
        grid=(t // tm,),
        in_specs=[
            pl.BlockSpec((tm, d), lambda i: (i, 0)),
            pl.BlockSpec((1, d), lambda i: (0, 0)),
            pl.BlockSpec((None, 6, d), lambda i: (row_fn(i), 0, 0)),
            pl.BlockSpec((n_exp, d), lambda i: (0, 0)),
            pl.BlockSpec((n_exp, 1), lambda i: (0, 0)),
            pl.BlockSpec((tm, tm), lambda i: (0, 0)),
        ],
        out_specs=[
            pl.BlockSpec((tm, d // 2), lambda i: (i, 0)),
            pl.BlockSpec((TOP_K, tm), lambda i: (0, i)),
            pl.BlockSpec((TOP_K, tm), lambda i: (0, i)),
            pl.BlockSpec((TOP_K, tm), lambda i: (0, i)),
            pl.BlockSpec((n_exp, 1), lambda i: (0, 0)),
        ],
        out_shape=[
            jax.ShapeDtypeStruct((t, d // 2), U32),
            jax.ShapeDtypeStruct((TOP_K, t), I32),
            jax.ShapeDtypeStruct((TOP_K, t), I32),
            jax.ShapeDtypeStruct((TOP_K, t), F32),
            jax.ShapeDtypeStruct((n_exp, 1), I32),
        ],
        scratch_shapes=[pltpu.VMEM((n_exp, 1), F32)],
        compiler_params=_cparams(("arbitrary",)),
        name="moe_route",
    )(x, nw.reshape(1, d), mod_l, w_router.T, bias.reshape(n_exp, 1), tri)


def _row_copy_ring(n, make_copy, sems):
    ring = min(DMA_RING, n)

    def issue(j, _):
        slot = j % ring

        @pl.when(j >= ring)
        def _():
            make_copy(j - ring, slot).wait()

        make_copy(j, slot).start()
        return 0

    lax.fori_loop(0, n, issue, 0)

    def drain(j, _):
        make_copy(j, j % ring).wait()
        return 0

    lax.fori_loop(n - ring, n, drain, 0)


def _dispatch_body(dest_ref, hp_ref, xs_in_ref, xs_ref, sems, *, tm):
    del xs_in_ref
    base = pl.program_id(0) * tm * TOP_K

    def make_copy(j, slot):
        t = j // TOP_K
        return pltpu.make_async_copy(hp_ref.at[pl.ds(t, 1)], xs_ref.at[pl.ds(dest_ref[base + j], 1)], sems.at[slot])

    _row_copy_ring(tm * TOP_K, make_copy, sems)


def _dispatch(dest_flat, hp, xs_buf, tm):
    t, half = hp.shape
    return pl.pallas_call(
        functools.partial(_dispatch_body, tm=tm),
        grid_spec=pltpu.PrefetchScalarGridSpec(
            num_scalar_prefetch=1,
            grid=(t // tm,),
            in_specs=[pl.BlockSpec((tm, half), lambda i, dest: (i, 0)), pl.BlockSpec(memory_space=pl.ANY)],
            out_specs=pl.BlockSpec(memory_space=pl.ANY),
            scratch_shapes=[pltpu.SemaphoreType.DMA((DMA_RING,))],
        ),
        out_shape=jax.ShapeDtypeStruct(xs_buf.shape, xs_buf.dtype),
        input_output_aliases={2: 0},
        compiler_params=_cparams(("arbitrary",)),
        name="moe_dispatch",
    )(dest_flat, hp, xs_buf)


def _expert_changed(i, be_ref):
    prev = be_ref[jnp.maximum(i - 1, 0)]
    return (i == 0) | (be_ref[i] != prev)


def _expert_up_body(be_ref, nu_ref, xs_ref, w1_ref, w3_ref, o_ref, w1b, w3b):
    i = pl.program_id(0)

    @pl.when(i < nu_ref[0])
    def _():
        @pl.when(_expert_changed(i, be_ref))
        def _():
            w1b[...] = w1_ref[...].astype(BF16)
            w3b[...] = w3_ref[...].astype(BF16)

        half = xs_ref.shape[1]
        xa, xb = _unpack_bf16_pair(xs_ref[...])
        a = (jnp.dot(xa, w1b[0:half, :], preferred_element_type=F32)
             + jnp.dot(xb, w1b[half:, :], preferred_element_type=F32))
        b = (jnp.dot(xa, w3b[0:half, :], preferred_element_type=F32)
             + jnp.dot(xb, w3b[half:, :], preferred_element_type=F32))
        o_ref[...] = (_silu(a) * b).astype(BF16)

    @pl.when(i >= nu_ref[0])
    def _():
        o_ref[...] = jnp.zeros_like(o_ref)


def _expert_up(block_e, n_used, xs, w1, w3, layer):
    rows, half = xs.shape
    _, _, d, ff = w1.shape
    n_blocks = rows // MOE_ROWS
    xrow = lambda i, be, nu: (jnp.minimum(i, nu[0] - 1), 0)
    wsel = lambda i, be, nu: (layer, be[i], 0, 0)
    return pl.pallas_call(
        _expert_up_body,
        grid_spec=pltpu.PrefetchScalarGridSpec(
            num_scalar_prefetch=2,
            grid=(n_blocks,),
            in_specs=[
                pl.BlockSpec((MOE_ROWS, half), xrow),
                pl.BlockSpec((None, None, d, ff), wsel),
                pl.BlockSpec((None, None, d, ff), wsel),
            ],
            out_specs=pl.BlockSpec((MOE_ROWS, ff), lambda i, be, nu: (i, 0)),
            scratch_shapes=[pltpu.VMEM((d, ff), BF16), pltpu.VMEM((d, ff), BF16)],
        ),
        out_shape=jax.ShapeDtypeStruct((rows, ff), BF16),
        compiler_params=_cparams(("arbitrary",)),
        name="moe_expert_up",
    )(block_e, n_used, xs, w1, w3)


def _expert_down_body(be_ref, nu_ref, h_ref, w2_ref, o_ref, w2b):
    i = pl.program_id(0)

    @pl.when(i < nu_ref[0])
    def _():
        @pl.when(_expert_changed(i, be_ref))
        def _():
            w2b[...] = w2_ref[...].astype(BF16)

        y = jnp.dot(h_ref[...], w2b[...], preferred_element_type=F32)
        half = y.shape[1] // 2
        o_ref[...] = _pack_bf16_pair(y[:, :half], y[:, half:])

    @pl.when(i >= nu_ref[0])
    def _():
        o_ref[...] = jnp.zeros_like(o_ref)


def _expert_down(block_e, n_used, hmid, w2, layer):
    rows, ff = hmid.shape
    d = w2.shape[-1]
    n_blocks = rows // MOE_ROWS
    return pl.pallas_call(
        _expert_down_body,
        grid_spec=pltpu.PrefetchScalarGridSpec(
            num_scalar_prefetch=2,
            grid=(n_blocks,),
            in_specs=[
                pl.BlockSpec((MOE_ROWS, ff), lambda i, be, nu: (jnp.minimum(i, nu[0] - 1), 0)),
                pl.BlockSpec((None, None, ff, d), lambda i, be, nu: (layer, be[i], 0, 0)),
            ],
            out_specs=pl.BlockSpec((MOE_ROWS, d // 2), lambda i, be, nu: (i, 0)),
            scratch_shapes=[pltpu.VMEM((ff, d), BF16)],
        ),
        out_shape=jax.ShapeDtypeStruct((rows, d // 2), U32),
        compiler_params=_cparams(("arbitrary",)),
        name="moe_expert_down",
    )(block_e, n_used, hmid, w2)


def _shared_up_body(hp_ref, w1_ref, w3_ref, o_ref):
    half = hp_ref.shape[1]
    xa, xb = _unpack_bf16_pair(hp_ref[...])
    a = (jnp.dot(xa, w1_ref[0:half, :], preferred_element_type=F32)
         + jnp.dot(xb, w1_ref[half:, :], preferred_element_type=F32))
    b = (jnp.dot(xa, w3_ref[0:half, :], preferred_element_type=F32)
         + jnp.dot(xb, w3_ref[half:, :], preferred_element_type=F32))
    o_ref[...] = (_silu(a) * b).astype(BF16)


def _shared_up(hp, ws1_bf16, ws3_bf16, tm):
    t, half = hp.shape
    d, ff = ws1_bf16.shape
    return pl.pallas_call(
        _shared_up_body,
        grid=(t // tm,),
        in_specs=[
            pl.BlockSpec((tm, half), lambda i: (i, 0)),
            pl.BlockSpec((d, ff), lambda i: (0, 0)),
            pl.BlockSpec((d, ff), lambda i: (0, 0)),
        ],
        out_specs=pl.BlockSpec((tm, ff), lambda i: (i, 0)),
        out_shape=jax.ShapeDtypeStruct((t, ff), BF16),
        compiler_params=_cparams(("parallel",)),
        name="moe_shared_up",
    )(hp, ws1_bf16, ws3_bf16)


def _combine_body(dest_ref, ys_ref, wgt_ref, hs_ref, ws2_ref, x_ref, mod_ref, o_ref, buf, sems, *, tm):
    base = pl.program_id(0) * tm * TOP_K

    def make_copy(j, slot):
        t = j // TOP_K
        k = j % TOP_K
        return pltpu.make_async_copy(ys_ref.at[pl.ds(dest_ref[base + j], 1)], buf.at[k, pl.ds(t, 1)], sems.at[slot])

    _row_copy_ring(tm * TOP_K, make_copy, sems)

    half = buf.shape[-1]
    acc_a = jnp.zeros((tm, half), F32)
    acc_b = jnp.zeros((tm, half), F32)
    for k in range(TOP_K):
        ya, yb = _unpack_bf16_pair(buf[k])
        wk = wgt_ref[:, k:k + 1]
        acc_a = acc_a + wk * ya.astype(F32)
        acc_b = acc_b + wk * yb.astype(F32)
    shared = jnp.dot(hs_ref[...], ws2_ref[...], preferred_element_type=F32)
    gate = mod_ref[5:6, :]
    o_ref[:, 0:half] = x_ref[:, 0:half] + gate[:, 0:half] * (acc_a + shared[:, 0:half])
    o_ref[:, half:] = x_ref[:, half:] + gate[:, half:] * (acc_b + shared[:, half:])


def _combine(dest_flat, ys, wgt_t, hs, ws2_bf16, x, mod_l, row_fn, tm):
    t, d = x.shape
    ff = hs.shape[1]
    return pl.pallas_call(
        functools.partial(_combine_body, tm=tm),
        grid_spec=pltpu.PrefetchScalarGridSpec(
            num_scalar_prefetch=1,
            grid=(t // tm,),
            in_specs=[
                pl.BlockSpec(memory_space=pl.ANY),
                pl.BlockSpec((tm, TOP_K), lambda i, dest: (i, 0)),
                pl.BlockSpec((tm, ff), lambda i, dest: (i, 0)),
                pl.BlockSpec((ff, d), lambda i, dest: (0, 0)),
                pl.BlockSpec((tm, d), lambda i, dest: (i, 0)),
                pl.BlockSpec((None, 6, d), lambda i, dest: (row_fn(i), 0, 0)),
            ],
            out_specs=pl.BlockSpec((tm, d), lambda i, dest: (i, 0)),
            scratch_shapes=[pltpu.VMEM((TOP_K, tm, d // 2), U32), pltpu.SemaphoreType.DMA((DMA_RING,))],
        ),
        out_shape=jax.ShapeDtypeStruct((t, d), F32),
        compiler_params=_cparams(("arbitrary",)),
        name="moe_combine",
    )(dest_flat, ys, wgt_t, hs, ws2_bf16, x, mod_l)


def _moe(x, p, experts, layer, mod_l, xs_buf, n_prompt, dec_seq):
    t, d = x.shape
    w1, w3, w2 = experts
    n_exp = w1.shape[1]
    tm_r = min(256, dec_seq)
    hp, eidx, rank, wgt, counts = _route(x, p['norm_ff'], mod_l, p['w_router'], p['router_bias'],
                                         _mod_row_fn(n_prompt // tm_r, dec_seq // tm_r), tm_r)
    counts = counts.reshape(n_exp)
    padded = (counts + MOE_ROWS - 1) // MOE_ROWS * MOE_ROWS
    pad_end = jnp.cumsum(padded)
    offsets = pad_end - padded
    n_blocks = xs_buf.shape[0] // MOE_ROWS
    block_e = jnp.minimum(jnp.searchsorted(pad_end, jnp.arange(n_blocks, dtype=I32) * MOE_ROWS, side='right'),
                          n_exp - 1).astype(I32)
    n_used = (pad_end[-1:] // MOE_ROWS).astype(I32)
    dest = (jnp.take(offsets, eidx) + rank).astype(I32)
    dest_flat = dest.T.reshape(t * TOP_K)
    xs = _dispatch(dest_flat, hp, xs_buf, tm_r)
    hmid = _expert_up(block_e, n_used, xs, w1, w3, layer)
    ys = _expert_down(block_e, n_used, hmid, w2, layer)
    hs = _shared_up(hp, p['ws1'].astype(BF16), p['ws3'].astype(BF16), min(512, dec_seq))
    tm_c = min(128, dec_seq)
    x = _combine(dest_flat, ys, wgt.T, hs, p['ws2'].astype(BF16), x, mod_l,
                 _mod_row_fn(n_prompt // tm_c, dec_seq // tm_c), tm_c)
    return x, xs


def _layer(x, mod_l, p, experts, layer, xs_buf, dims, ctx, rope):
    n_batch, seq, dec_batch, dec_seq = dims
    n_prompt = n_batch * seq
    t, d = x.shape
    ssm_w = p['w_glu'].shape[0]
    conv_w = p['w_pw'].shape[0]
    att_w = d - ssm_w - conv_w
    in_w = p['w_in'].shape[1]
    tm = min(512, dec_seq)
    row_fn = _mod_row_fn(n_prompt // tm, dec_seq // tm)

    proj = _in_proj(x, p['norm_mix'], mod_l, p['w_in'].astype(BF16), row_fn, tm)

    lam_init = 0.8 - 0.6 * math.exp(-0.3 * layer)
    lam = (jnp.exp(jnp.sum(p['lam_q1'].astype(F32) * p['lam_k1'].astype(F32)))
           - jnp.exp(jnp.sum(p['lam_q2'].astype(F32) * p['lam_k2'].astype(F32))) + lam_init)
    att_kw = dict(att_width=att_w, out_scale=1.0 - lam_init)
    att_p, k_p, v_p = _attention(proj, lam, p['q_norm'], p['k_norm'], p['subln'], n_batch=n_batch, seq=seq,
                                 row_block0=0, **att_kw)
    ck, cv, h0_re, h0_im = ctx
    att_s = _attention(proj, lam, p['q_norm'], p['k_norm'], p['subln'], n_batch=dec_batch, seq=dec_seq,
                       row_block0=n_prompt // dec_seq, ctx=(ck, cv), rope=rope, **att_kw)

    tables = _ssm_tables(p['ssm_a_re'], p['ssm_a_im'], p['ssm_log_dt'], p['ssm_b_re'], p['ssm_b_im'],
                         p['ssm_c_re'], p['ssm_c_im'])
    ucol0 = 3 * att_w
    y_p, fin_re, fin_im = _ssm_scan(proj.reshape(t // seq, seq, in_w), tables, n_batch=n_batch, seq=seq,
                                    batch_block0=0, ucol0=ucol0, nb=SUBLANES)
    y_s = _ssm_scan(proj.reshape(t // dec_seq, dec_seq, in_w), tables, n_batch=dec_batch, seq=dec_seq,
                    batch_block0=(n_prompt // dec_seq) // dec_batch, ucol0=ucol0, nb=dec_batch,
                    h0=(h0_re, h0_im))
    tm_s = min(256, dec_seq)
    glu = (p['ssm_d'], p['w_glu'].astype(BF16), p['b_glu'])
    s_p = _ssm_out(proj, y_p.reshape(2, n_prompt, ssm_w), *glu, row_block0=0, ucol0=ucol0, tm=tm_s)
    s_s = _ssm_out(proj, y_s.reshape(2, dec_batch * dec_seq, ssm_w), *glu, row_block0=n_prompt // tm_s,
                   ucol0=ucol0, tm=tm_s)

    conv = (p['conv_dw'], p['conv_dw_b'], p['conv_ln_g'], p['conv_ln_b'], p['w_pw'].astype(BF16), p['b_pw'])
    zcol0 = ucol0 + ssm_w
    c_p = _conv_mixer(proj, *conv, n_batch=n_batch, seq=seq, row_block0=0, zcol0=zcol0)
    c_s = _conv_mixer(proj, *conv, n_batch=dec_batch, seq=dec_seq, row_block0=n_prompt // dec_seq, zcol0=zcol0)

    cat = lambda a, b: jnp.concatenate([a, b], axis=0)
    x = _out_proj(cat(att_p, att_s), cat(s_p, s_s), cat(c_p, c_s), p['w_out'].astype(BF16), x, mod_l, row_fn, tm)
    x, xs_buf = _moe(x, p, experts, layer, mod_l, xs_buf, n_prompt, dec_seq)
    return x, xs_buf, (k_p, v_p, fin_re, fin_im)


def kernel(x_prompt, x_sample, cache_k, cache_v, state_ssm_re, state_ssm_im, c, c_ctx, w_ada, b_ada, norm_mix, norm_ff, w_in, w_out, q_norm, k_norm, lam_q1, lam_k1, lam_q2, lam_k2, subln, ssm_a_re, ssm_a_im, ssm_log_dt, ssm_b_re, ssm_b_im, ssm_c_re, ssm_c_im, ssm_d, w_glu, b_glu, conv_dw, conv_dw_b, conv_ln_g, conv_ln_b, w_pw, b_pw, w_router, router_bias, w1, w3, w2, ws1, ws3, ws2):
    n_batch, seq, d = x_prompt.shape
    dec_batch, dec_seq, _ = x_sample.shape
    depth = w_in.shape[0]
    n_prompt, n_dec = n_batch * seq, dec_batch * dec_seq
    t = n_prompt + n_dec
    head_d = q_norm.shape[-1]
    n_groups, n_state = ssm_a_re.shape[-2:]
    gs = SSM_SLICE_GROUPS
    n_exp = w1.shape[1]
    assert dec_batch + 1 <= SUBLANES and n_prompt % dec_seq == 0 and n_batch % SUBLANES == 0
    assert (n_prompt // dec_seq) % dec_batch == 0

    cmat = jnp.zeros((SUBLANES, d), F32).at[0].set(c_ctx).at[1:1 + dec_batch].set(c)
    mod = _adaln(cmat, w_ada, b_ada).reshape(depth, SUBLANES, 6, d)

    x = jnp.concatenate([x_prompt.reshape(n_prompt, d), x_sample.reshape(n_dec, d)], axis=0)
    rope = _rope_tables(dec_seq, head_d)
    n_blocks = -(-(t * TOP_K) // MOE_ROWS) + n_exp
    xs_buf = jnp.zeros((n_blocks * MOE_ROWS, d // 2), U32)

    def slice_state(st, l):
        return st[:, l].reshape(dec_batch, 2, n_groups // gs, gs * n_state).transpose(1, 2, 0, 3)

    ks_, vs_, srs, sis = [], [], [], []
    for l in range(depth):
        p = {
            'norm_mix': norm_mix[l], 'norm_ff': norm_ff[l], 'w_in': w_in[l], 'w_out': w_out[l],
            'q_norm': q_norm[l], 'k_norm': k_norm[l], 'lam_q1': lam_q1[l], 'lam_k1': lam_k1[l],
            'lam_q2': lam_q2[l], 'lam_k2': lam_k2[l], 'subln': subln[l], 'ssm_a_re': ssm_a_re[l],
            'ssm_a_im': ssm_a_im[l], 'ssm_log_dt': ssm_log_dt[l], 'ssm_b_re': ssm_b_re[l], 'ssm_b_im': ssm_b_im[l],
            'ssm_c_re': ssm_c_re[l], 'ssm_c_im': ssm_c_im[l], 'ssm_d': ssm_d[l], 'w_glu': w_glu[l],
            'b_glu': b_glu[l], 'conv_dw': conv_dw[l], 'conv_dw_b': conv_dw_b[l], 'conv_ln_g': conv_ln_g[l],
            'conv_ln_b': conv_ln_b[l], 'w_pw': w_pw[l], 'b_pw': b_pw[l], 'w_router': w_router[l],
            'router_bias': router_bias[l], 'ws1': ws1[l], 'ws3': ws3[l], 'ws2': ws2[l],
        }
        past = cache_k.shape[2]
        ctx = (cache_k[:, l].reshape(dec_batch, past, -1), cache_v[:, l].reshape(dec_batch, past, -1),
               slice_state(state_ssm_re, l), slice_state(state_ssm_im, l))
        x, xs_buf, (k_l, v_l, fr, fi) = _layer(x, mod[l], p, (w1, w3, w2), l, xs_buf,
                                               (n_batch, seq, dec_batch, dec_seq), ctx, rope)
        n_heads = k_l.shape[1] // (2 * head_d)
        ks_.append(k_l.reshape(n_batch, seq, n_heads, 2, head_d))
        vs_.append(v_l.reshape(n_batch, seq, n_heads, 2 * head_d))
        unslice = lambda st: st.transpose(2, 0, 1, 3).reshape(n_batch, 2, n_groups, n_state)
        srs.append(unslice(fr))
        sis.append(unslice(fi))

    y_prompt = x[:n_prompt].reshape(n_batch, seq, d)
    y_sample = x[n_prompt:].reshape(dec_batch, dec_seq, d)
    return (y_prompt, y_sample, jnp.stack(ks_, axis=1), jnp.stack(vs_, axis=1),
            jnp.stack(srs, axis=1), jnp.stack(sis, axis=1))
```

```python
import functools
import math

import jax
import jax.numpy as jnp
from jax import lax
from jax.experimental import pallas as pl
from jax.experimental.pallas import tpu as pltpu

F32 = jnp.float32
BF16 = jnp.bfloat16
U32 = jnp.uint32
I32 = jnp.int32

EPS = 1e-6
GRID_W = 64
ROPE_BASE = 10000.0
TOP_K = 8
N_EXPERT_GROUPS = 8
TOPK_GROUPS = 4
ROUTED_SCALE = 2.5

LANES = 128
SUBLANES = 8
VMEM_LIMIT = 56 * 1024 * 1024
SSM_SLICE_GROUPS = 8
MOE_ROWS = 256
RING_TOKENS = 32
RANK_BITS = 20


def _cparams(sem, vmem=VMEM_LIMIT):
    return pltpu.CompilerParams(dimension_semantics=sem, vmem_limit_bytes=vmem)


def _sigmoid(x):
    return 1.0 / (1.0 + jnp.exp(-x))


def _silu(x):
    return x * _sigmoid(x)


def _adaln_body(c_ref, w_ref, b_ref, o_ref):
    c = c_ref[...]
    s = _silu(c).astype(BF16)
    o_ref[0] = jnp.dot(s, w_ref[0].astype(BF16), preferred_element_type=F32) + b_ref[0]


def _adaln(cmat, w_ada, b_ada):
    depth, d, n = w_ada.shape
    tn = 512
    return pl.pallas_call(
        _adaln_body,
        grid=(depth, n // tn),
        in_specs=[
            pl.BlockSpec((SUBLANES, d), lambda l, j: (0, 0)),
            pl.BlockSpec((1, d, tn), lambda l, j: (l, 0, j)),
            pl.BlockSpec((1, 1, tn), lambda l, j: (l, 0, j)),
        ],
        out_specs=pl.BlockSpec((1, SUBLANES, tn), lambda l, j: (l, 0, j)),
        out_shape=jax.ShapeDtypeStruct((depth, SUBLANES, n), F32),
        compiler_params=_cparams(("parallel", "parallel")),
        name="adaln",
    )(cmat, w_ada, b_ada.reshape(depth, 1, n))


def _norm_mod(x, nw, shift, scale):
    ms = jnp.mean(x * x, axis=-1, keepdims=True)
    y = x * lax.rsqrt(ms + EPS) * nw
    return y * (1.0 + scale) + shift


def _pack_bf16_pair(a, b):
    ab = pltpu.bitcast(a.astype(BF16).astype(F32), U32)
    bb = pltpu.bitcast(b.astype(BF16).astype(F32), U32)
    return (ab & jnp.uint32(0xFFFF0000)) | (bb >> 16)


def _unpack_bf16_pair(p):
    a = pltpu.bitcast(p & jnp.uint32(0xFFFF0000), F32).astype(BF16)
    b = pltpu.bitcast(p << 16, F32).astype(BF16)
    return a, b


def _mod_row_fn(n_prompt_tiles, tiles_per_dec_batch):
    def row(i):
        return jnp.where(i < n_prompt_tiles, 0, 1 + (i - n_prompt_tiles) // tiles_per_dec_batch)
    return row


def _in_proj_body(x_ref, nw_ref, mod_ref, w_ref, o_ref, h_scr):
    @pl.when(pl.program_id(1) == 0)
    def _():
        h = _norm_mod(x_ref[...], nw_ref[...], mod_ref[0:1, :], mod_ref[1:2, :])
        h_scr[...] = h.astype(BF16)

    o_ref[...] = jnp.dot(h_scr[...], w_ref[...], preferred_element_type=F32)


def _in_proj(x, nw, mod_l, w_bf16, row_fn, tm):
    t, d = x.shape
    n = w_bf16.shape[1]
    tn = 512 if n % 512 == 0 else LANES
    return pl.pallas_call(
        _in_proj_body,
        grid=(t // tm, n // tn),
        in_specs=[
            pl.BlockSpec((tm, d), lambda i, j: (i, 0)),
            pl.BlockSpec((1, d), lambda i, j: (0, 0)),
            pl.BlockSpec((None, 6, d), lambda i, j: (row_fn(i), 0, 0)),
            pl.BlockSpec((d, tn), lambda i, j: (0, j)),
        ],
        out_specs=pl.BlockSpec((tm, tn), lambda i, j: (i, j)),
        out_shape=jax.ShapeDtypeStruct((t, n), F32),
        scratch_shapes=[pltpu.VMEM((tm, d), BF16)],
        compiler_params=_cparams(("parallel", "arbitrary")),
        name="in_proj",
    )(x, nw.reshape(1, d), mod_l, w_bf16)


def _half_norm(x, w):
    lane = lax.broadcasted_iota(I32, (1, LANES), 1)
    lo = lane < (LANES // 2)
    x2 = x * x
    s_lo = jnp.sum(jnp.where(lo, x2, 0.0), axis=-1, keepdims=True)
    s_hi = jnp.sum(jnp.where(lo, 0.0, x2), axis=-1, keepdims=True)
    ss = jnp.where(lo, s_lo, s_hi)
    return x * lax.rsqrt(ss * (2.0 / LANES) + EPS) * w


def _rope(x, cos, sa, sb):
    return x * cos + pltpu.roll(x, LANES - 16, 1) * sa + pltpu.roll(x, 16, 1) * sb


def _softmax(s):
    m = jnp.max(s, axis=-1, keepdims=True)
    e = jnp.exp(s - m)
    return e / jnp.sum(e, axis=-1, keepdims=True)


def _attn_body(*refs, seq, ctx_len, heads, q_chunk, out_scale):
    has_ctx = ctx_len > 0
    if has_ctx:
        (lam_ref, q_ref, k_ref, v_ref, qn_ref, kn_ref, sub_ref, ck_ref, cv_ref, cos_ref, sa_ref, sb_ref,
         _, att_ref, kcat, vcat) = refs
    else:
        lam_ref, q_ref, k_ref, v_ref, qn_ref, kn_ref, sub_ref, _, _, _, att_ref, ko_ref, vo_ref = refs
    lam = lam_ref[0]
    lane = lax.broadcasted_iota(I32, (1, LANES), 1)
    lo = lane < (LANES // 2)
    scale = (LANES // 2) ** -0.5
    for j in range(heads):
        sl = slice(j * LANES, (j + 1) * LANES)
        kn = _half_norm(k_ref[:, sl], kn_ref[...])
        if has_ctx:
            kn = _rope(kn, cos_ref[...], sa_ref[...], sb_ref[...])
            kcat[0:ctx_len, :] = ck_ref[:, sl].astype(BF16)
            kcat[ctx_len:, :] = kn.astype(BF16)
            vcat[0:ctx_len, :] = cv_ref[:, sl].astype(BF16)
            vcat[ctx_len:, :] = v_ref[:, sl].astype(BF16)
            kfull = kcat[...]
            vfull = vcat[...]
        else:
            ko_ref[:, sl] = kn
            vo_ref[:, sl] = v_ref[:, sl]
            kfull = kn.astype(BF16)
            vfull = v_ref[:, sl].astype(BF16)
        for c in range(seq // q_chunk):
            rows = slice(c * q_chunk, (c + 1) * q_chunk)
            qn = _half_norm(q_ref[rows, sl], qn_ref[...])
            if has_ctx:
                qn = _rope(qn, cos_ref[rows, :], sa_ref[rows, :], sb_ref[rows, :])
            qn = qn * scale
            q1 = jnp.where(lo, qn, 0.0).astype(BF16)
            q2 = jnp.where(lo, 0.0, qn).astype(BF16)
            dn = (((1,), (1,)), ((), ()))
            p1 = _softmax(lax.dot_general(q1, kfull, dn, preferred_element_type=F32))
            p2 = _softmax(lax.dot_general(q2, kfull, dn, preferred_element_type=F32))
            pd = (p1 - lam * p2).astype(BF16)
            o = jnp.dot(pd, vfull, preferred_element_type=F32)
            o = o * lax.rsqrt(jnp.mean(o * o, axis=-1, keepdims=True) + EPS) * sub_ref[...] * out_scale
            att_ref[rows, sl] = o.astype(BF16)


def _attention(proj, lam, qn_w, kn_w, sub_w, *, n_batch, seq, row_block0, att_width, out_scale,
               ctx=None, rope=None, into=None, cache=None):
    n_heads = att_width // LANES
    heads = min(n_heads, 4) if ctx is not None else n_heads
    hw = heads * LANES
    kcol = att_width // hw
    q_chunk = min(seq, 256)
    rep = lambda w: jnp.tile(w.reshape(1, -1), (1, LANES // w.shape[-1])).astype(F32)
    smem = pl.BlockSpec(memory_space=pltpu.SMEM)
    vec = pl.BlockSpec((1, LANES), lambda b, h: (0, 0))
    in_specs = [
        smem,
        pl.BlockSpec((seq, hw), lambda b, h: (row_block0 + b, h)),
        pl.BlockSpec((seq, hw), lambda b, h: (row_block0 + b, kcol + h)),
        pl.BlockSpec((seq, hw), lambda b, h: (row_block0 + b, 2 * kcol + h)),
        vec, vec, vec,
    ]
    args = [lam.reshape(1), proj, proj, proj, rep(qn_w), rep(kn_w), rep(sub_w)]
    att_spec = pl.BlockSpec((seq, hw), lambda b, h: (row_block0 + b, h))
    att_shape = jax.ShapeDtypeStruct((proj.shape[0], att_width), BF16)
    if ctx is None:
        ctx_len = 0
        layer, k_cache, v_cache = cache
        kv_spec = pl.BlockSpec((None, None, seq, hw), lambda b, h: (b, layer, 0, h))
        kv_shape = jax.ShapeDtypeStruct(k_cache.shape, F32)
        in_specs += [pl.BlockSpec(memory_space=pl.ANY)] * 3
        args += [into, k_cache, v_cache]
        aliases = {len(args) - 3: 0, len(args) - 2: 1, len(args) - 1: 2}
        out_specs = [att_spec, kv_spec, kv_spec]
        out_shape = [att_shape, kv_shape, kv_shape]
        scratch = []
    else:
        ck, cv = ctx
        ctx_len = ck.shape[1]
        cspec = pl.BlockSpec((None, ctx_len, hw), lambda b, h: (b, 0, h))
        tab = pl.BlockSpec((seq, LANES), lambda b, h: (0, 0))
        in_specs += [cspec, cspec, tab, tab, tab, pl.BlockSpec(memory_space=pl.ANY)]
        args += [ck, cv, *rope, into]
        aliases = {len(args) - 1: 0}
        out_specs = att_spec
        out_shape = att_shape
        scratch = [pltpu.VMEM((ctx_len + seq, LANES), BF16), pltpu.VMEM((ctx_len + seq, LANES), BF16)]
    body = functools.partial(_attn_body, seq=seq, ctx_len=ctx_len, heads=heads, q_chunk=q_chunk,
                             out_scale=out_scale)
    return pl.pallas_call(
        body,
        grid=(n_batch, n_heads // heads),
        in_specs=in_specs,
        out_specs=out_specs,
        out_shape=out_shape,
        scratch_shapes=scratch,
        input_output_aliases=aliases,
        compiler_params=_cparams(("parallel", "parallel")),
        name="attn_ctx" if ctx is None else "attn_dec",
    )(*args)


def _rope_tables(n_tok, head_d):
    axis_dim = head_d // 2
    n_freq = axis_dim // 2
    pos = jnp.arange(n_tok)
    row = (pos // GRID_W).astype(F32)
    col = (pos % GRID_W).astype(F32)
    inv = ROPE_BASE ** (-jnp.arange(n_freq, dtype=F32) / n_freq)
    ang = jnp.stack([row[:, None] * inv, col[:, None] * inv], axis=1)
    cos = jnp.broadcast_to(jnp.cos(ang)[:, :, None, :], (n_tok, 2, 2, n_freq))
    sin = jnp.broadcast_to(jnp.sin(ang)[:, :, None, :], (n_tok, 2, 2, n_freq))
    first = (jnp.arange(2) == 0)[None, None, :, None]
    sa = jnp.where(first, -sin, 0.0)
    sb = jnp.where(first, 0.0, sin)
    flat = lambda t: jnp.tile(t.reshape(n_tok, head_d), (1, LANES // head_d))
    return flat(cos), flat(sa), flat(sb)


def _ssm_tables(a_re, a_im, log_dt, b_re, b_im, c_re, c_im):
    _, g, p = a_re.shape
    hch = b_re.shape[-1]
    gs = SSM_SLICE_GROUPS
    s = g // gs
    dt = jnp.exp(log_dt.astype(F32))[..., None]
    lr = a_re.astype(F32) * dt
    li = a_im.astype(F32) * dt
    mag = jnp.exp(lr)
    ar, ai = mag * jnp.cos(li), mag * jnp.sin(li)
    zr, zi = ar - 1.0, ai
    den = lr * lr + li * li
    fr = (zr * lr + zi * li) / den * dt
    fi = (zi * lr - zr * li) / den * dt
    bbr = fr[..., None] * b_re - fi[..., None] * b_im
    bbi = fr[..., None] * b_im + fi[..., None] * b_re
    eye = jnp.eye(gs, dtype=F32)

    def bdiag_b(m):
        m = m.reshape(2, s, gs, p, hch)
        return jnp.einsum('dsgph,gk->dsghkp', m, eye).reshape(2, s, gs * hch, gs * p)

    def bdiag_c(m):
        m = m.reshape(2, s, gs, hch, p)
        return jnp.einsum('dsghp,gk->dsgpkh', m, eye).reshape(2, s, gs * p, gs * hch)

    bblk = jnp.concatenate([bdiag_b(bbr), bdiag_b(bbi)], axis=-1).astype(BF16)
    cblk = jnp.concatenate([bdiag_c(c_re.astype(F32)), bdiag_c(-c_im.astype(F32))], axis=-2).astype(BF16)
    a = jnp.stack([ar.reshape(2, s, gs * p), ai.reshape(2, s, gs * p)], axis=2)
    return a, bblk, cblk


def _ssm_body(*refs, nb, tc, gw, has_h0):
    if has_h0:
        u_ref, b_ref, c_ref, a_ref, h0r_ref, h0i_ref, _, y_ref, s_scr, h_scr, ut_scr, y_scr = refs
    else:
        u_ref, b_ref, c_ref, a_ref, _, y_ref, fr_ref, fi_ref, s_scr, h_scr, ut_scr, y_scr = refs
    d = pl.program_id(0)
    c = pl.program_id(3)

    nt = gw // LANES
    sub = SUBLANES // nb
    nblk = tc // sub
    tile = lambda j: slice(j * LANES, (j + 1) * LANES)

    @pl.when(c == 0)
    def _():
        for j in range(nt):
            if has_h0:
                h_scr[j] = jnp.concatenate([h0r_ref[:, tile(j)]] * sub, axis=0)
                h_scr[nt + j] = jnp.concatenate([h0i_ref[:, tile(j)]] * sub, axis=0)
            else:
                h_scr[j] = jnp.zeros((SUBLANES, LANES), F32)
                h_scr[nt + j] = jnp.zeros((SUBLANES, LANES), F32)

    for b in range(nb):
        ut_scr[pl.ds(b, tc, stride=nb), :] = u_ref[b]
    bu = jnp.dot(ut_scr[...].astype(BF16), b_ref[...], preferred_element_type=F32)
    for j in range(2 * nt):
        s_scr[j] = bu[:, tile(j)]
    ar = [jnp.broadcast_to(a_ref[0:1, tile(j)], (SUBLANES, LANES)) for j in range(nt)]
    ai = [jnp.broadcast_to(a_ref[1:2, tile(j)], (SUBLANES, LANES)) for j in range(nt)]
    row = lax.broadcasted_iota(I32, (SUBLANES, LANES), 0)

    def run(rev):
        shift = (SUBLANES - nb) if rev else nb
        order = list(range(sub - 1, -1, -1)) if rev else list(range(sub))

        def step(i, carry):
            blk = (nblk - 1 - i) if rev else i
            rows = pl.ds(pl.multiple_of(blk * SUBLANES, SUBLANES), SUBLANES)
            res = []
            for j in range(nt):
                bur = s_scr[j, rows, :]
                bui = s_scr[nt + j, rows, :]
                hr, hi = carry[j], carry[nt + j]
                if sub == 1:
                    hr, hi = ar[j] * hr - ai[j] * hi + bur, ar[j] * hi + ai[j] * hr + bui
                    outr, outi = hr, hi
                else:
                    outr = outi = None
                    for s in order:
                        pr = pltpu.roll(hr, shift, 0)
                        pi = pltpu.roll(hi, shift, 0)
                        hr, hi = ar[j] * pr - ai[j] * pi + bur, ar[j] * pi + ai[j] * pr + bui
                        if outr is None:
                            outr, outi = hr, hi
                        else:
                            m = (row >= s * nb) & (row < (s + 1) * nb)
                            outr = jnp.where(m, hr, outr)
                            outi = jnp.where(m, hi, outi)
                s_scr[j, rows, :] = outr
                s_scr[nt + j, rows, :] = outi
                res.append((hr, hi))
            return tuple(r[0] for r in res) + tuple(r[1] for r in res)

        h = lax.fori_loop(0, nblk, step, tuple(h_scr[j] for j in range(2 * nt)))
        for j in range(2 * nt):
            h_scr[j] = h[j]
        if not has_h0:
            lo = 0 if rev else (sub - 1) * nb
            for j in range(nt):
                fr_ref[:, tile(j)] = h[j][lo:lo + nb, :]
                fi_ref[:, tile(j)] = h[nt + j][lo:lo + nb, :]

    @pl.when(d == 0)
    def _():
        run(False)

    @pl.when(d == 1)
    def _():
        run(True)

    hs = jnp.concatenate([s_scr[j] for j in range(2 * nt)], axis=-1).astype(BF16)
    y_scr[...] = jnp.dot(hs, c_ref[...], preferred_element_type=F32)
    for b in range(nb):
        y_ref[b] = y_scr[pl.ds(b, tc, stride=nb), :]


def _ssm_scan(proj3, tables, *, n_batch, seq, batch_block0, ucol0, nb, h0=None, into=None):
    a, bblk, cblk = tables
    _, s, _, gw = a.shape
    tc = min(seq, 256)
    n_chunks = seq // tc
    ucol = ucol0 // LANES
    ssm_w = s * LANES
    cidx = lambda d, c: jnp.where(d == 0, c, n_chunks - 1 - c)
    in_specs = [
        pl.BlockSpec((nb, tc, LANES), lambda d, b, g, c: (batch_block0 + b, cidx(d, c), ucol + g)),
        pl.BlockSpec((None, None, LANES, 2 * gw), lambda d, b, g, c: (d, g, 0, 0)),
        pl.BlockSpec((None, None, 2 * gw, LANES), lambda d, b, g, c: (d, g, 0, 0)),
        pl.BlockSpec((None, None, 2, gw), lambda d, b, g, c: (d, g, 0, 0)),
    ]
    args = [proj3, bblk, cblk, a]
    y_spec = pl.BlockSpec((None, nb, tc, LANES), lambda d, b, g, c: (d, batch_block0 + b, cidx(d, c), g))
    y_shape = jax.ShapeDtypeStruct((2, proj3.shape[0], seq, ssm_w), F32)
    st_spec = pl.BlockSpec((None, None, nb, gw), lambda d, b, g, c: (d, g, b, 0))
    if h0 is None:
        out_specs = [y_spec, st_spec, st_spec]
        st_shape = jax.ShapeDtypeStruct((2, s, n_batch, gw), F32)
        out_shape = [y_shape, st_shape, st_shape]
    else:
        in_specs += [st_spec, st_spec]
        args += list(h0)
        out_specs = y_spec
        out_shape = y_shape
    in_specs.append(pl.BlockSpec(memory_space=pl.ANY))
    args.append(into.reshape(y_shape.shape))
    aliases = {len(args) - 1: 0}
    body = functools.partial(_ssm_body, nb=nb, tc=tc, gw=gw, has_h0=h0 is not None)
    return pl.pallas_call(
        body,
        grid=(2, n_batch // nb, s, n_chunks),
        in_specs=in_specs,
        out_specs=out_specs,
        out_shape=out_shape,
        scratch_shapes=[pltpu.VMEM((2 * gw // LANES, nb * tc, LANES), F32),
                        pltpu.VMEM((2 * gw // LANES, SUBLANES, LANES), F32),
                        pltpu.VMEM((nb * tc, LANES), F32),
                        pltpu.VMEM((nb * tc, LANES), F32)],
        input_output_aliases=aliases,
        compiler_params=_cparams(("parallel", "parallel", "parallel", "arbitrary")),
        name="ssm_scan",
    )(*args)


def _gelu_tanh(x):
    return 0.5 * x * (1.0 + jnp.tanh(math.sqrt(2.0 / math.pi) * (x + 0.044715 * x * x * x)))


def _ssm_out_body(u_ref, y_ref, d_ref, w_ref, b_ref, o_ref):
    y = d_ref[...] * u_ref[...] + y_ref[0] + y_ref[1]
    g = _gelu_tanh(y).astype(BF16)
    z = jnp.dot(g, w_ref[...], preferred_element_type=F32) + b_ref[...]
    o_ref[...] = (g.astype(F32) * _sigmoid(z)).astype(BF16)


def _ssm_out(proj, y2, d_vec, w_glu_bf16, b_glu, *, ucol0, tm):
    _, rows, w = y2.shape
    ucol = ucol0 // w
    return pl.pallas_call(
        _ssm_out_body,
        grid=(rows // tm,),
        in_specs=[
            pl.BlockSpec((tm, w), lambda i: (i, ucol)),
            pl.BlockSpec((2, tm, w), lambda i: (0, i, 0)),
            pl.BlockSpec((1, w), lambda i: (0, 0)),
            pl.BlockSpec((w, w), lambda i: (0, 0)),
            pl.BlockSpec((1, w), lambda i: (0, 0)),
        ],
        out_specs=pl.BlockSpec((tm, w), lambda i: (i, 0)),
        out_shape=jax.ShapeDtypeStruct((rows, w), BF16),
        compiler_params=_cparams(("parallel",)),
        name="ssm_out",
    )(proj, y2, d_vec.reshape(1, w), w_glu_bf16, b_glu.reshape(1, w))


def _conv_body(a_ref, g_ref, dw_ref, dwb_ref, lng_ref, lnb_ref, w_ref, b_ref, _, o_ref, pad_scr, hs_scr, *, seq,
               taps, halo, row_chunk):
    w = a_ref.shape[-1]
    pad_scr[0:halo, :] = jnp.zeros((halo, w), F32)
    pad_scr[halo + seq:, :] = jnp.zeros((halo, w), F32)
    pad_scr[halo:halo + seq, :] = a_ref[...] * _sigmoid(g_ref[...])
    first = halo - taps // 2

    def chunk(r, _):
        r0 = pl.multiple_of(r * row_chunk, row_chunk)
        accs = []
        for lt in range(w // LANES):
            cols = slice(lt * LANES, (lt + 1) * LANES)
            win = pad_scr[pl.ds(r0, row_chunk + 2 * halo), cols]
            acc = jnp.zeros((row_chunk, LANES), F32) + dwb_ref[:, cols]
            for k in range(taps):
                acc = acc + win[first + k:first + k + row_chunk, :] * dw_ref[k:k + 1, cols]
            accs.append(acc)
        acc = jnp.concatenate(accs, axis=-1)
        mu = jnp.mean(acc, axis=-1, keepdims=True)
        xc = acc - mu
        var = jnp.mean(xc * xc, axis=-1, keepdims=True)
        hn = xc * lax.rsqrt(var + EPS) * lng_ref[...] + lnb_ref[...]
        hs_scr[pl.ds(r0, row_chunk), :] = _silu(hn).astype(BF16)
        return 0

    lax.fori_loop(0, seq // row_chunk, chunk, 0)
    out = jnp.dot(hs_scr[...], w_ref[...], preferred_element_type=F32) + b_ref[...]
    o_ref[...] = out.astype(BF16)


def _conv_mixer(proj, dw, dw_b, ln_g, ln_b, w_pw_bf16, b_pw, *, n_batch, seq, row_block0, zcol0, into):
    taps, w = dw.shape
    halo = 16
    acol = zcol0 // w
    row = lambda v: v.reshape(1, w)
    vec = pl.BlockSpec((1, w), lambda b: (0, 0))
    body = functools.partial(_conv_body, seq=seq, taps=taps, halo=halo, row_chunk=32)
    in_specs = [
        pl.BlockSpec((seq, w), lambda b: (row_block0 + b, acol)),
        pl.BlockSpec((seq, w), lambda b: (row_block0 + b, acol + 1)),
        pl.BlockSpec((taps, w), lambda b: (0, 0)),
        vec, vec, vec,
        pl.BlockSpec((w, w), lambda b: (0, 0)),
        vec,
        pl.BlockSpec(memory_space=pl.ANY),
    ]
    args = [proj, proj, dw, row(dw_b), row(ln_g), row(ln_b), w_pw_bf16, row(b_pw), into]
    aliases = {len(args) - 1: 0}
    return pl.pallas_call(
        body,
        grid=(n_batch,),
        in_specs=in_specs,
        out_specs=pl.BlockSpec((seq, w), lambda b: (row_block0 + b, 0)),
        out_shape=jax.ShapeDtypeStruct((proj.shape[0], w), BF16),
        scratch_shapes=[pltpu.VMEM((seq + 2 * halo, w), F32), pltpu.VMEM((seq, w), BF16)],
        input_output_aliases=aliases,
        compiler_params=_cparams(("parallel",)),
        name="conv_mixer",
    )(*args)


def _out_proj_body(att_ref, s_ref, c_ref, w_ref, x_ref, mod_ref, o_ref):
    wa = att_ref.shape[-1]
    ws = s_ref.shape[-1]
    mix = jnp.dot(att_ref[...], w_ref[0:wa, :], preferred_element_type=F32)
    mix = mix + jnp.dot(s_ref[...], w_ref[wa:wa + ws, :], preferred_element_type=F32)
    mix = mix + jnp.dot(c_ref[...], w_ref[wa + ws:, :], preferred_element_type=F32)
    o_ref[...] = x_ref[...] + mod_ref[2:3, :] * mix


def _out_proj(att, s_out, c_out, w_bf16, x, mod_l, row_fn, tm):
    t, d = x.shape
    tn = 512
    return pl.pallas_call(
        _out_proj_body,
        grid=(t // tm, d // tn),
        in_specs=[
            pl.BlockSpec((tm, att.shape[1]), lambda i, j: (i, 0)),
            pl.BlockSpec((tm, s_out.shape[1]), lambda i, j: (i, 0)),
            pl.BlockSpec((tm, c_out.shape[1]), lambda i, j: (i, 0)),
            pl.BlockSpec((d, tn), lambda i, j: (0, j)),
            pl.BlockSpec((tm, tn), lambda i, j: (i, j)),
            pl.BlockSpec((None, 6, tn), lambda i, j: (row_fn(i), 0, j)),
        ],
        out_specs=pl.BlockSpec((tm, tn), lambda i, j: (i, j)),
        out_shape=jax.ShapeDtypeStruct((t, d), F32),
        compiler_params=_cparams(("parallel", "parallel")),
        name="out_proj",
    )(att, s_out, c_out, w_bf16, x, mod_l)


def _route_body(x_ref, nw_ref, mod_ref, wrt_ref, bias_ref, tri_ref, hp_ref, code_ref, wgt_ref, cnt_ref, cnt_scr):
    i = pl.program_id(0)

    @pl.when(i == 0)
    def _():
        cnt_scr[...] = jnp.zeros_like(cnt_scr)

    h = _norm_mod(x_ref[...], nw_ref[...], mod_ref[3:4, :], mod_ref[4:5, :])
    half = h.shape[1] // 2
    hp_ref[...] = _pack_bf16_pair(h[:, :half], h[:, half:])

    n_exp = wrt_ref.shape[0]
    tm = h.shape[0]
    per_group = n_exp // N_EXPERT_GROUPS
    logits = lax.dot_general(wrt_ref[...], h, (((1,), (1,)), ((), ())), precision=lax.Precision.HIGHEST,
                             preferred_element_type=F32)
    scores = _sigmoid(logits)
    sel = scores + bias_ref[...]
    neg = -jnp.inf

    sel3 = sel.reshape(N_EXPERT_GROUPS, per_group, tm)
    pos3 = lax.broadcasted_iota(I32, sel3.shape, 1)
    m1 = jnp.max(sel3, axis=1, keepdims=True)
    first1 = jnp.min(jnp.where(sel3 == m1, pos3, per_group), axis=1, keepdims=True)
    m2 = jnp.max(jnp.where(pos3 == first1, neg, sel3), axis=1, keepdims=True)
    gscore = m1 + m2

    gpos = lax.broadcasted_iota(I32, gscore.shape, 0)
    gkeep = jnp.zeros(gscore.shape, F32)
    for _ in range(TOPK_GROUPS):
        gm = jnp.max(gscore, axis=0, keepdims=True)
        gfirst = jnp.min(jnp.where(gscore == gm, gpos, N_EXPERT_GROUPS), axis=0, keepdims=True)
        hit = gpos == gfirst
        gkeep = jnp.where(hit, 1.0, gkeep)
        gscore = jnp.where(hit, neg, gscore)
    keep3 = jnp.broadcast_to(gkeep, sel3.shape) > 0.5
    cand = jnp.where(keep3, sel3, neg).reshape(n_exp, tm)

    epos = lax.broadcasted_iota(I32, cand.shape, 0)
    picks = []
    chosen = jnp.zeros(cand.shape, F32)
    for _ in range(TOP_K):
        em = jnp.max(cand, axis=0, keepdims=True)
        efirst = jnp.min(jnp.where(cand == em, epos, n_exp), axis=0, keepdims=True)
        hit = epos == efirst
        chosen = jnp.where(hit, 1.0, chosen)
        cand = jnp.where(hit, neg, cand)
        picks.append(efirst)

    gate = chosen * scores
    gate = gate / jnp.sum(gate, axis=0, keepdims=True) * ROUTED_SCALE

    rank = jnp.dot(chosen.astype(BF16), tri_ref[...], preferred_element_type=F32) + cnt_scr[...]
    cnt_scr[...] = cnt_scr[...] + jnp.sum(chosen, axis=1, keepdims=True)
    cnt_ref[...] = cnt_scr[...].astype(I32)

    for k in range(TOP_K):
        hit = epos == picks[k]
        rank_k = jnp.sum(jnp.where(hit, rank, 0.0), axis=0, keepdims=True).astype(I32)
        code_ref[k:k + 1, :] = picks[k] * (1 << RANK_BITS) + rank_k
        wgt_ref[k:k + 1, :] = jnp.sum(jnp.where(hit, gate, 0.0), axis=0, keepdims=True)


def _route(x, nw, mod_l, w_router, bias, row_fn, tm):
    t, d = x.shape
    n_exp = w_router.shape[1]
    tri = (jnp.arange(tm)[:, None] < jnp.arange(tm)[None, :]).astype(BF16)
    return pl.pallas_call(
        _route_body,
        grid=(t // tm,),
        in_specs=[
            pl.BlockSpec((tm, d), lambda i: (i, 0)),
            pl.BlockSpec((1, d), lambda i: (0, 0)),
            pl.BlockSpec((None, 6, d), lambda i: (row_fn(i), 0, 0)),
            pl.BlockSpec((n_exp, d), lambda i: (0, 0)),
            pl.BlockSpec((n_exp, 1), lambda i: (0, 0)),
            pl.BlockSpec((tm, tm), lambda i: (0, 0)),
        ],
        out_specs=[
            pl.BlockSpec((tm, d // 2), lambda i: (i, 0)),
            pl.BlockSpec((TOP_K, tm), lambda i: (0, i)),
            pl.BlockSpec((TOP_K, tm), lambda i: (0, i)),
            pl.BlockSpec((n_exp, 1), lambda i: (0, 0)),
        ],
        out_shape=[
            jax.ShapeDtypeStruct((t, d // 2), U32),
            jax.ShapeDtypeStruct((TOP_K, t), I32),
            jax.ShapeDtypeStruct((TOP_K, t), F32),
            jax.ShapeDtypeStruct((n_exp, 1), I32),
        ],
        scratch_shapes=[pltpu.VMEM((n_exp, 1), F32)],
        compiler_params=_cparams(("arbitrary",)),
        name="moe_route",
    )(x, nw.reshape(1, d), mod_l, w_router.T, bias.reshape(n_exp, 1), tri)


def _row_copy_ring(n_tok, make_copy, wait_copy):
    ring = min(RING_TOKENS, n_tok)
    assert ring & (ring - 1) == 0
    slot0 = lambda t: (t & (ring - 1)) * TOP_K

    def start_token(t):
        for k in range(TOP_K):
            make_copy(t, k, slot0(t) + k).start()

    def wait_token(t):
        for k in range(TOP_K):
            wait_copy(slot0(t) + k).wait()

    def fill(t, _):
        start_token(t)
        return 0

    def steady(t, _):
        wait_token(t - ring)
        start_token(t)
        return 0

    def drain(t, _):
        wait_token(t)
        return 0

    lax.fori_loop(0, ring, fill, 0)
    lax.fori_loop(ring, n_tok, steady, 0)
    lax.fori_loop(n_tok - ring, n_tok, drain, 0)


def _dest_row(code_ref, off_ref, idx):
    code = code_ref[idx]
    return off_ref[code >> RANK_BITS] + (code & ((1 << RANK_BITS) - 1))


def _dispatch_body(code_ref, off_ref, hp_ref, xs_in_ref, xs_ref, sems, *, tm, n_tok):
    del xs_in_ref
    base = pl.program_id(0) * tm

    def make_copy(t, k, slot):
        dest = _dest_row(code_ref, off_ref, k * n_tok + base + t)
        return pltpu.make_async_copy(hp_ref.at[pl.ds(t, 1)], xs_ref.at[pl.ds(dest, 1)], sems.at[slot])

    def wait_copy(slot):
        return pltpu.make_async_copy(hp_ref.at[pl.ds(0, 1)], xs_ref.at[pl.ds(0, 1)], sems.at[slot])

    _row_copy_ring(tm, make_copy, wait_copy)


def _dispatch(code_flat, offsets, hp, xs_buf, tm):
    t, half = hp.shape
    return pl.pallas_call(
        functools.partial(_dispatch_body, tm=tm, n_tok=t),
        grid_spec=pltpu.PrefetchScalarGridSpec(
            num_scalar_prefetch=2,
            grid=(t // tm,),
            in_specs=[pl.BlockSpec((tm, half), lambda i, code, off: (i, 0)), pl.BlockSpec(memory_space=pl.ANY)],
            out_specs=pl.BlockSpec(memory_space=pl.ANY),
            scratch_shapes=[pltpu.SemaphoreType.DMA((RING_TOKENS * TOP_K,))],
        ),
        out_shape=jax.ShapeDtypeStruct(xs_buf.shape, xs_buf.dtype),
        input_output_aliases={3: 0},
        compiler_params=_cparams(("arbitrary",)),
        name="moe_dispatch",
    )(code_flat, offsets, hp, xs_buf)


SCHED_EXPERT, SCHED_FIRST, SCHED_NEXT, SCHED_SLOT = range(4)


def _expert_schedule(block_e, n_used):
    n_blocks = block_e.shape[0]
    idx = jnp.arange(n_blocks, dtype=I32)
    used = idx < n_used[0]
    prev = jnp.concatenate([jnp.full((1,), -1, I32), block_e[:-1]])
    first = used & (block_e != prev)
    slot = (jnp.cumsum(first.astype(I32)) - 1) & 1
    later_first = jnp.where(first, idx, n_blocks)
    next_first = lax.cummin(later_first[::-1])[::-1]
    next_first = jnp.concatenate([next_first[1:], jnp.full((1,), n_blocks, I32)])
    nxt = jnp.where(next_first < n_blocks, block_e[jnp.minimum(next_first, n_blocks - 1)], -1)
    return jnp.stack([block_e, first.astype(I32), nxt.astype(I32), slot.astype(I32)]).astype(I32)


def _stream_expert_weights(i, sched_ref, copies, on_arrival):
    @pl.when(i == 0)
    def _():
        for c in copies(sched_ref[SCHED_EXPERT, 0], 0):
            c.start()

    @pl.when(sched_ref[SCHED_FIRST, i] == 1)
    def _():
        slot = sched_ref[SCHED_SLOT, i]
        for c in copies(sched_ref[SCHED_EXPERT, i], slot):
            c.wait()
        nxt = sched_ref[SCHED_NEXT, i]

        @pl.when(nxt >= 0)
        def _():
            for c in copies(nxt, 1 - slot):
                c.start()

        on_arrival(slot)


def _expert_up_body(sched_ref, nu_ref, xs_ref, w1_hbm, w3_hbm, o_ref, st1, st3, w1b, w3b, sems, *, layer):
    i = pl.program_id(0)

    def copies(e, slot):
        return (pltpu.make_async_copy(w1_hbm.at[layer, e], st1.at[slot], sems.at[0, slot]),
                pltpu.make_async_copy(w3_hbm.at[layer, e], st3.at[slot], sems.at[1, slot]))

    def on_arrival(slot):
        w1b[...] = st1[slot].astype(BF16)
        w3b[...] = st3[slot].astype(BF16)

    _stream_expert_weights(i, sched_ref, copies, on_arrival)

    @pl.when(i < nu_ref[0])
    def _():
        half = xs_ref.shape[1]
        xa, xb = _unpack_bf16_pair(xs_ref[...])
        a = (jnp.dot(xa, w1b[0:half, :], preferred_element_type=F32)
             + jnp.dot(xb, w1b[half:, :], preferred_element_type=F32))
        b = (jnp.dot(xa, w3b[0:half, :], preferred_element_type=F32)
             + jnp.dot(xb, w3b[half:, :], preferred_element_type=F32))
        o_ref[...] = (_silu(a) * b).astype(BF16)

    @pl.when(i >= nu_ref[0])
    def _():
        o_ref[...] = jnp.zeros_like(o_ref)


def _expert_up(sched, n_used, xs, w1, w3, layer):
    rows, half = xs.shape
    _, _, d, ff = w1.shape
    n_blocks = rows // MOE_ROWS
    hbm = pl.BlockSpec(memory_space=pl.ANY)
    return pl.pallas_call(
        functools.partial(_expert_up_body, layer=layer),
        grid_spec=pltpu.PrefetchScalarGridSpec(
            num_scalar_prefetch=2,
            grid=(n_blocks,),
            in_specs=[pl.BlockSpec((MOE_ROWS, half), lambda i, sc, nu: (jnp.minimum(i, nu[0] - 1), 0)), hbm, hbm],
            out_specs=pl.BlockSpec((MOE_ROWS, ff), lambda i, sc, nu: (i, 0)),
            scratch_shapes=[pltpu.VMEM((2, d, ff), F32), pltpu.VMEM((2, d, ff), F32),
                            pltpu.VMEM((d, ff), BF16), pltpu.VMEM((d, ff), BF16),
                            pltpu.SemaphoreType.DMA((2, 2))],
        ),
        out_shape=jax.ShapeDtypeStruct((rows, ff), BF16),
        compiler_params=_cparams(("arbitrary",)),
        name="moe_expert_up",
    )(sched, n_used, xs, w1, w3)


def _expert_down_body(sched_ref, nu_ref, h_ref, w2_hbm, o_ref, st2, w2b, sems, *, layer):
    i = pl.program_id(0)

    def copies(e, slot):
        return (pltpu.make_async_copy(w2_hbm.at[layer, e], st2.at[slot], sems.at[slot]),)

    def on_arrival(slot):
        w2b[...] = st2[slot].astype(BF16)

    _stream_expert_weights(i, sched_ref, copies, on_arrival)

    @pl.when(i < nu_ref[0])
    def _():
        y = jnp.dot(h_ref[...], w2b[...], preferred_element_type=F32)
        half = y.shape[1] // 2
        o_ref[...] = _pack_bf16_pair(y[:, :half], y[:, half:])

    @pl.when(i >= nu_ref[0])
    def _():
        o_ref[...] = jnp.zeros_like(o_ref)


def _expert_down(sched, n_used, hmid, w2, layer):
    rows, ff = hmid.shape
    d = w2.shape[-1]
    n_blocks = rows // MOE_ROWS
    return pl.pallas_call(
        functools.partial(_expert_down_body, layer=layer),
        grid_spec=pltpu.PrefetchScalarGridSpec(
            num_scalar_prefetch=2,
            grid=(n_blocks,),
            in_specs=[
                pl.BlockSpec((MOE_ROWS, ff), lambda i, sc, nu: (jnp.minimum(i, nu[0] - 1), 0)),
                pl.BlockSpec(memory_space=pl.ANY),
            ],
            out_specs=pl.BlockSpec((MOE_ROWS, d // 2), lambda i, sc, nu: (i, 0)),
            scratch_shapes=[pltpu.VMEM((2, ff, d), F32), pltpu.VMEM((ff, d), BF16), pltpu.SemaphoreType.DMA((2,))],
        ),
        out_shape=jax.ShapeDtypeStruct((rows, d // 2), U32),
        compiler_params=_cparams(("arbitrary",)),
        name="moe_expert_down",
    )(sched, n_used, hmid, w2)


def _shared_up_body(hp_ref, w1_ref, w3_ref, o_ref):
    half = hp_ref.shape[1]
    xa, xb = _unpack_bf16_pair(hp_ref[...])
    a = (jnp.dot(xa, w1_ref[0:half, :], preferred_element_type=F32)
         + jnp.dot(xb, w1_ref[half:, :], preferred_element_type=F32))
    b = (jnp.dot(xa, w3_ref[0:half, :], preferred_element_type=F32)
         + jnp.dot(xb, w3_ref[half:, :], preferred_element_type=F32))
    o_ref[...] = (_silu(a) * b).astype(BF16)


def _shared_up(hp, ws1_bf16, ws3_bf16, tm):
    t, half = hp.shape
    d, ff = ws1_bf16.shape
    return pl.pallas_call(
        _shared_up_body,
        grid=(t // tm,),
        in_specs=[
            pl.BlockSpec((tm, half), lambda i: (i, 0)),
            pl.BlockSpec((d, ff), lambda i: (0, 0)),
            pl.BlockSpec((d, ff), lambda i: (0, 0)),
        ],
        out_specs=pl.BlockSpec((tm, ff), lambda i: (i, 0)),
        out_shape=jax.ShapeDtypeStruct((t, ff), BF16),
        compiler_params=_cparams(("parallel",)),
        name="moe_shared_up",
    )(hp, ws1_bf16, ws3_bf16)


def _combine_body(code_ref, off_ref, ys_ref, wgt_ref, hs_ref, ws2_ref, x_ref, mod_ref, o_ref, buf, sems, *, tm,
                  n_tok):
    base = pl.program_id(0) * tm

    def make_copy(t, k, slot):
        src = _dest_row(code_ref, off_ref, k * n_tok + base + t)
        return pltpu.make_async_copy(ys_ref.at[pl.ds(src, 1)], buf.at[k, pl.ds(t, 1)], sems.at[slot])

    def wait_copy(slot):
        return pltpu.make_async_copy(ys_ref.at[pl.ds(0, 1)], buf.at[0, pl.ds(0, 1)], sems.at[slot])

    _row_copy_ring(tm, make_copy, wait_copy)

    half = buf.shape[-1]
    acc_a = jnp.zeros((tm, half), F32)
    acc_b = jnp.zeros((tm, half), F32)
    for k in range(TOP_K):
        ya, yb = _unpack_bf16_pair(buf[k])
        wk = wgt_ref[:, k:k + 1]
        acc_a = acc_a + wk * ya.astype(F32)
        acc_b = acc_b + wk * yb.astype(F32)
    shared = jnp.dot(hs_ref[...], ws2_ref[...], preferred_element_type=F32)
    gate = mod_ref[5:6, :]
    o_ref[:, 0:half] = x_ref[:, 0:half] + gate[:, 0:half] * (acc_a + shared[:, 0:half])
    o_ref[:, half:] = x_ref[:, half:] + gate[:, half:] * (acc_b + shared[:, half:])


def _combine(code_flat, offsets, ys, wgt_t, hs, ws2_bf16, x, mod_l, row_fn, tm):
    t, d = x.shape
    ff = hs.shape[1]
    return pl.pallas_call(
        functools.partial(_combine_body, tm=tm, n_tok=t),
        grid_spec=pltpu.PrefetchScalarGridSpec(
            num_scalar_prefetch=2,
            grid=(t // tm,),
            in_specs=[
                pl.BlockSpec(memory_space=pl.ANY),
                pl.BlockSpec((tm, TOP_K), lambda i, code, off: (i, 0)),
                pl.BlockSpec((tm, ff), lambda i, code, off: (i, 0)),
                pl.BlockSpec((ff, d), lambda i, code, off: (0, 0)),
                pl.BlockSpec((tm, d), lambda i, code, off: (i, 0)),
                pl.BlockSpec((None, 6, d), lambda i, code, off: (row_fn(i), 0, 0)),
            ],
            out_specs=pl.BlockSpec((tm, d), lambda i, code, off: (i, 0)),
            scratch_shapes=[pltpu.VMEM((TOP_K, tm, d // 2), U32), pltpu.SemaphoreType.DMA((RING_TOKENS * TOP_K,))],
        ),
        out_shape=jax.ShapeDtypeStruct((t, d), F32),
        compiler_params=_cparams(("arbitrary",)),
        name="moe_combine",
    )(code_flat, offsets, ys, wgt_t, hs, ws2_bf16, x, mod_l)


def _moe(x, p, experts, layer, mod_l, xs_buf, n_prompt, dec_seq):
    t, d = x.shape
    w1, w3, w2 = experts
    n_exp = w1.shape[1]
    tm_r = min(256, dec_seq)
    assert t * TOP_K < (1 << RANK_BITS)
    hp, code, wgt, counts = _route(x, p['norm_ff'], mod_l, p['w_router'], p['router_bias'],
                                   _mod_row_fn(n_prompt // tm_r, dec_seq // tm_r), tm_r)
    counts = counts.reshape(n_exp)
    padded = (counts + MOE_ROWS - 1) // MOE_ROWS * MOE_ROWS
    pad_end = jnp.cumsum(padded)
    offsets = (pad_end - padded).astype(I32)
    n_blocks = xs_buf.shape[0] // MOE_ROWS
    block_start = jnp.arange(n_blocks, dtype=I32) * MOE_ROWS
    block_e = jnp.minimum(jnp.sum(pad_end[None, :] <= block_start[:, None], axis=1), n_exp - 1).astype(I32)
    n_used = (pad_end[-1:] // MOE_ROWS).astype(I32)
    code_flat = code.reshape(TOP_K * t)
    sched = _expert_schedule(block_e, n_used)
    xs = _dispatch(code_flat, offsets, hp, xs_buf, tm_r)
    hmid = _expert_up(sched, n_used, xs, w1, w3, layer)
    ys = _expert_down(sched, n_used, hmid, w2, layer)
    hs = _shared_up(hp, p['ws1'].astype(BF16), p['ws3'].astype(BF16), min(512, dec_seq))
    tm_c = min(128, dec_seq)
    x = _combine(code_flat, offsets, ys, wgt.T, hs, p['ws2'].astype(BF16), x, mod_l,
                 _mod_row_fn(n_prompt // tm_c, dec_seq // tm_c), tm_c)
    return x, xs


def _layer(x, mod_l, p, experts, layer, depth, xs_buf, kv_bufs, dims, ctx, rope):
    n_batch, seq, dec_batch, dec_seq = dims
    n_prompt = n_batch * seq
    t, d = x.shape
    ssm_w = p['w_glu'].shape[0]
    conv_w = p['w_pw'].shape[0]
    att_w = d - ssm_w - conv_w
    in_w = p['w_in'].shape[1]
    tm = min(512, dec_seq)
    row_fn = _mod_row_fn(n_prompt // tm, dec_seq // tm)

    proj = _in_proj(x, p['norm_mix'], mod_l, p['w_in'].astype(BF16), row_fn, tm)

    lam_init = 0.8 - 0.6 * math.exp(-0.3 * layer)
    lam = (jnp.exp(jnp.sum(p['lam_q1'].astype(F32) * p['lam_k1'].astype(F32)))
           - jnp.exp(jnp.sum(p['lam_q2'].astype(F32) * p['lam_k2'].astype(F32))) + lam_init)
    att_kw = dict(att_width=att_w, out_scale=1.0 - lam_init)
    att, k_p, v_p = _attention(proj, lam, p['q_norm'], p['k_norm'], p['subln'], n_batch=n_batch, seq=seq,
                               row_block0=0, into=jnp.zeros((t, att_w), BF16), cache=(layer, *kv_bufs), **att_kw)
    ck, cv, h0_re, h0_im = ctx
    att = _attention(proj, lam, p['q_norm'], p['k_norm'], p['subln'], n_batch=dec_batch, seq=dec_seq,
                     row_block0=n_prompt // dec_seq, ctx=(ck, cv), rope=rope, into=att, **att_kw)

    tables = _ssm_tables(p['ssm_a_re'], p['ssm_a_im'], p['ssm_log_dt'], p['ssm_b_re'], p['ssm_b_im'],
                         p['ssm_c_re'], p['ssm_c_im'])
    ucol0 = 3 * att_w
    y, fin_re, fin_im = _ssm_scan(proj.reshape(t // seq, seq, in_w), tables, n_batch=n_batch, seq=seq,
                                  batch_block0=0, ucol0=ucol0, nb=SUBLANES, into=jnp.zeros((2, t, ssm_w), F32))
    y = _ssm_scan(proj.reshape(t // dec_seq, dec_seq, in_w), tables, n_batch=dec_batch, seq=dec_seq,
                  batch_block0=(n_prompt // dec_seq) // dec_batch, ucol0=ucol0, nb=dec_batch,
                  h0=(h0_re, h0_im), into=y)
    s_out = _ssm_out(proj, y.reshape(2, t, ssm_w), p['ssm_d'], p['w_glu'].astype(BF16), p['b_glu'], ucol0=ucol0,
                     tm=min(256, dec_seq))

    conv = (p['conv_dw'], p['conv_dw_b'], p['conv_ln_g'], p['conv_ln_b'], p['w_pw'].astype(BF16), p['b_pw'])
    zcol0 = ucol0 + ssm_w
    c_out = _conv_mixer(proj, *conv, n_batch=n_batch, seq=seq, row_block0=0, zcol0=zcol0,
                        into=jnp.zeros((t, conv_w), BF16))
    c_out = _conv_mixer(proj, *conv, n_batch=dec_batch, seq=dec_seq, row_block0=n_prompt // dec_seq, zcol0=zcol0,
                        into=c_out)

    x = _out_proj(att, s_out, c_out, p['w_out'].astype(BF16), x, mod_l, row_fn, tm)
    x, xs_buf = _moe(x, p, experts, layer, mod_l, xs_buf, n_prompt, dec_seq)
    return x, xs_buf, (k_p, v_p, fin_re, fin_im)


def kernel(x_prompt, x_sample, cache_k, cache_v, state_ssm_re, state_ssm_im, c, c_ctx, w_ada, b_ada, norm_mix, norm_ff, w_in, w_out, q_norm, k_norm, lam_q1, lam_k1, lam_q2, lam_k2, subln, ssm_a_re, ssm_a_im, ssm_log_dt, ssm_b_re, ssm_b_im, ssm_c_re, ssm_c_im, ssm_d, w_glu, b_glu, conv_dw, conv_dw_b, conv_ln_g, conv_ln_b, w_pw, b_pw, w_router, router_bias, w1, w3, w2, ws1, ws3, ws2):
    n_batch, seq, d = x_prompt.shape
    dec_batch, dec_seq, _ = x_sample.shape
    depth = w_in.shape[0]
    n_prompt, n_dec = n_batch * seq, dec_batch * dec_seq
    t = n_prompt + n_dec
    head_d = q_norm.shape[-1]
    n_groups, n_state = ssm_a_re.shape[-2:]
    gs = SSM_SLICE_GROUPS
    n_exp = w1.shape[1]
    assert dec_batch + 1 <= SUBLANES and n_prompt % dec_seq == 0 and n_batch % SUBLANES == 0
    assert (n_prompt // dec_seq) % dec_batch == 0

    cmat = jnp.zeros((SUBLANES, d), F32).at[0].set(c_ctx).at[1:1 + dec_batch].set(c)
    mod = _adaln(cmat, w_ada, b_ada).reshape(depth, SUBLANES, 6, d)

    x = jnp.concatenate([x_prompt.reshape(n_prompt, d), x_sample.reshape(n_dec, d)], axis=0)
    rope = _rope_tables(dec_seq, head_d)
    n_blocks = -(-(t * TOP_K) // MOE_ROWS) + n_exp
    xs_buf = jnp.zeros((n_blocks * MOE_ROWS, d // 2), U32)

    def slice_state(st, l):
        return st[:, l].reshape(dec_batch, 2, n_groups // gs, gs * n_state).transpose(1, 2, 0, 3)

    att_w = d - w_glu.shape[-1] - w_pw.shape[-1]
    kv_bufs = (jnp.zeros((n_batch, depth, seq, att_w), F32), jnp.zeros((n_batch, depth, seq, att_w), F32))
    srs, sis = [], []
    for l in range(depth):
        p = {
            'norm_mix': norm_mix[l], 'norm_ff': norm_ff[l], 'w_in': w_in[l], 'w_out': w_out[l],
            'q_norm': q_norm[l], 'k_norm': k_norm[l], 'lam_q1': lam_q1[l], 'lam_k1': lam_k1[l],
            'lam_q2': lam_q2[l], 'lam_k2': lam_k2[l], 'subln': subln[l], 'ssm_a_re': ssm_a_re[l],
            'ssm_a_im': ssm_a_im[l], 'ssm_log_dt': ssm_log_dt[l], 'ssm_b_re': ssm_b_re[l], 'ssm_b_im': ssm_b_im[l],
            'ssm_c_re': ssm_c_re[l], 'ssm_c_im': ssm_c_im[l], 'ssm_d': ssm_d[l], 'w_glu': w_glu[l],
            'b_glu': b_glu[l], 'conv_dw': conv_dw[l], 'conv_dw_b': conv_dw_b[l], 'conv_ln_g': conv_ln_g[l],
            'conv_ln_b': conv_ln_b[l], 'w_pw': w_pw[l], 'b_pw': b_pw[l], 'w_router': w_router[l],
            'router_bias': router_bias[l], 'ws1': ws1[l], 'ws3': ws3[l], 'ws2': ws2[l],
        }
        past = cache_k.shape[2]
        ctx = (cache_k[:, l].reshape(dec_batch, past, -1), cache_v[:, l].reshape(dec_batch, past, -1),
               slice_state(state_ssm_re, l), slice_state(state_ssm_im, l))
        x, xs_buf, (k_all, v_all, fr, fi) = _layer(x, mod[l], p, (w1, w3, w2), l, depth, xs_buf, kv_bufs,
                                                   (n_batch, seq, dec_batch, dec_seq), ctx, rope)
        kv_bufs = (k_all, v_all)
        unslice = lambda st: st.transpose(2, 0, 1, 3).reshape(n_batch, 2, n_groups, n_state)
        srs.append(unslice(fr))
        sis.append(unslice(fi))

    y_prompt = x[:n_prompt].reshape(n_batch, seq, d)
    y_sample = x[n_prompt:].reshape(dec_batch, dec_seq, d)
    n_heads = k_all.shape[-1] // (2 * head_d)
    return (y_prompt, y_sample, k_all.reshape(n_batch, depth, seq, n_heads, 2, head_d),
            v_all.reshape(n_batch, depth, seq, n_heads, 2 * head_d),
            jnp.stack(srs, axis=1), jnp.stack(sis, axis=1))
```
